```python
import math
import jax, jax.numpy as jnp
from jax import lax
import numpy as np

D_MODEL = 1024
BATCH = 2
SEQ = 8192
DEPTH = 2
DEC_BATCH = 4
DEC_SEQ = 8192
PAST_LEN = 128

CHUNK = 128
A_WIDTH = 1024
A_GROUPS = 8
A_GROUP_DIM = A_WIDTH // A_GROUPS
B_WIDTH = 1024
B_GROUP_DIM = 16
B_GROUPS = B_WIDTH // B_GROUP_DIM
B_STATE = 64
N_BRANCH = 2
D_FF = 4 * D_MODEL
IN_WIDTH = 2 * A_WIDTH + B_WIDTH + N_BRANCH * D_MODEL
EPS = 1e-6
DT_MIN = 1e-3
DT_MAX = 1e-1

kernel_name = "hybrid_gmlp_s5_bidir_encoder"


def rmsnorm(x, g):
    xf = x.astype(jnp.float32)
    y = xf * lax.rsqrt(jnp.mean(xf * xf, axis=-1, keepdims=True) + EPS) * g.astype(jnp.float32)
    return y.astype(x.dtype)


def spatial_gating(u, v, norm_v, w_s, b_s):
    bsz, s, _ = v.shape
    v = rmsnorm(v, norm_v).reshape(bsz, s // CHUNK, CHUNK, A_GROUPS, A_GROUP_DIM)
    mixed = jnp.einsum('gpq,bnqgc->bnpgc', w_s, v) + b_s.T[None, None, :, :, None]
    return u * mixed.reshape(bsz, s, A_WIDTH)


def s5_direction(u, lam_re, lam_im, log_dt, b_re, b_im, c_re, c_im):
    lam = lax.complex(lam_re.astype(jnp.float32), lam_im.astype(jnp.float32))
    dt = jnp.exp(log_dt.astype(jnp.float32))[:, None]
    a_bar = jnp.exp(lam * dt)
    b_mat = lax.complex(b_re.astype(jnp.float32), b_im.astype(jnp.float32))
    b_bar = ((a_bar - 1.0) / lam)[:, :, None] * b_mat
    bu = jnp.einsum('gph,bsgh->bsgp', b_bar, u.astype(b_bar.dtype))
    a_seq = jnp.broadcast_to(a_bar, bu.shape)

    def combine(left, right):
        a_l, s_l = left
        a_r, s_r = right
        return a_l * a_r, a_r * s_l + s_r

    _, states = lax.associative_scan(combine, (a_seq, bu), axis=1)
    c_mat = lax.complex(c_re.astype(jnp.float32), c_im.astype(jnp.float32))
    return jnp.einsum('ghp,bsgp->bsgh', c_mat, states).real


def s5_branch(u, l, p):
    bsz, s, _ = u.shape
    ug = u.astype(jnp.float32).reshape(bsz, s, B_GROUPS, B_GROUP_DIM)
    y_f = s5_direction(ug, p['lam_re'][l, 0], p['lam_im'][l, 0], p['log_dt'][l, 0],
                       p['b_re'][l, 0], p['b_im'][l, 0], p['c_re'][l, 0], p['c_im'][l, 0])
    y_b = jnp.flip(s5_direction(jnp.flip(ug, axis=1), p['lam_re'][l, 1], p['lam_im'][l, 1],
                                p['log_dt'][l, 1], p['b_re'][l, 1], p['b_im'][l, 1],
                                p['c_re'][l, 1], p['c_im'][l, 1]), axis=1)
    y = (y_f + y_b).reshape(bsz, s, B_WIDTH) + p['d_skip'][l].astype(jnp.float32) * ug.reshape(bsz, s, B_WIDTH)
    z = jax.nn.gelu(y).astype(u.dtype)
    glu = z @ p['w_glu'][l]
    g1, g2 = jnp.split(glu, 2, axis=-1)
    return g1 * jax.nn.sigmoid(g2)


def layer(x, l, p):
    h = rmsnorm(x, p['norm_pre_mix'][l])
    proj = h @ p['w_in'][l]
    u_a, v_a, u_b, gate_logits = jnp.split(
        proj, [A_WIDTH, 2 * A_WIDTH, 2 * A_WIDTH + B_WIDTH], axis=-1)
    a = spatial_gating(jax.nn.gelu(u_a), jax.nn.gelu(v_a), p['norm_v'][l],
                       p['w_s'][l], p['b_s'][l]) @ p['w_out_a'][l]
    b = s5_branch(u_b, l, p)
    g_a, g_b = jnp.split(jax.nn.sigmoid(gate_logits), 2, axis=-1)
    mix = (g_a * a + g_b * b) @ p['w_o'][l]
    x = x + rmsnorm(mix, p['norm_post_mix'][l])
    h = rmsnorm(x, p['norm_pre_ff'][l])
    f = jnp.square(jax.nn.relu(h @ p['w_ff1'][l])) @ p['w_ff2'][l]
    return x + rmsnorm(f, p['norm_post_ff'][l])


def trunk(x, p):
    for l in range(DEPTH):
        x = layer(x, l, p)
    return x


def setup_inputs(seed: int = 0) -> dict:
    key = jax.random.key(seed)
    k = jax.random.split(key, 24)
    f32 = jnp.float32

    def nrm(kk, shape, scale):
        return jax.random.normal(kk, shape, f32) * scale

    n_idx = jnp.arange(B_STATE, dtype=f32)
    lam_re = -0.5 + nrm(k[10], (DEPTH, 2, B_GROUPS, B_STATE), 0.01)
    lam_im = math.pi * n_idx + nrm(k[11], (DEPTH, 2, B_GROUPS, B_STATE), 0.01)
    log_dt = jax.random.uniform(k[12], (DEPTH, 2, B_GROUPS), f32,
                                math.log(DT_MIN), math.log(DT_MAX))
    return {
        'x_prompt': jax.random.normal(k[0], (BATCH, SEQ, D_MODEL), f32),
        'x_sample': jax.random.normal(k[1], (DEC_BATCH, DEC_SEQ, D_MODEL), f32),
        'norm_pre_mix': 1.0 + nrm(k[2], (DEPTH, D_MODEL), 0.02),
        'w_in': nrm(k[3], (DEPTH, D_MODEL, IN_WIDTH), D_MODEL ** -0.5),
        'norm_v': 1.0 + nrm(k[4], (DEPTH, A_WIDTH), 0.02),
        'w_s': nrm(k[5], (DEPTH, A_GROUPS, CHUNK, CHUNK), CHUNK ** -0.5),
        'b_s': 1.0 + nrm(k[6], (DEPTH, A_GROUPS, CHUNK), 0.02),
        'w_out_a': nrm(k[7], (DEPTH, A_WIDTH, D_MODEL), A_WIDTH ** -0.5),
        'lam_re': lam_re,
        'lam_im': lam_im,
        'log_dt': log_dt,
        'b_re': nrm(k[13], (DEPTH, 2, B_GROUPS, B_STATE, B_GROUP_DIM), (2 * B_GROUP_DIM) ** -0.5),
        'b_im': nrm(k[14], (DEPTH, 2, B_GROUPS, B_STATE, B_GROUP_DIM), (2 * B_GROUP_DIM) ** -0.5),
        'c_re': nrm(k[15], (DEPTH, 2, B_GROUPS, B_GROUP_DIM, B_STATE), (2 * B_STATE) ** -0.5),
        'c_im': nrm(k[16], (DEPTH, 2, B_GROUPS, B_GROUP_DIM, B_STATE), (2 * B_STATE) ** -0.5),
        'd_skip': nrm(k[17], (DEPTH, B_WIDTH), 1.0),
        'w_glu': nrm(k[18], (DEPTH, B_WIDTH, 2 * D_MODEL), B_WIDTH ** -0.5),
        'w_o': nrm(k[19], (DEPTH, D_MODEL, D_MODEL), D_MODEL ** -0.5),
        'norm_post_mix': 1.0 + nrm(k[20], (DEPTH, D_MODEL), 0.02),
        'norm_pre_ff': 1.0 + nrm(k[21], (DEPTH, D_MODEL), 0.02),
        'w_ff1': nrm(k[22], (DEPTH, D_MODEL, D_FF), D_MODEL ** -0.5),
        'w_ff2': nrm(k[23], (DEPTH, D_FF, D_MODEL), D_FF ** -0.5),
        'norm_post_ff': 1.0 + nrm(k[9], (DEPTH, D_MODEL), 0.02),
    }


def reference(x_prompt, x_sample, norm_pre_mix, w_in, norm_v, w_s, b_s, w_out_a,
              lam_re, lam_im, log_dt, b_re, b_im, c_re, c_im, d_skip, w_glu, w_o,
              norm_post_mix, norm_pre_ff, w_ff1, w_ff2, norm_post_ff):
    p = {
        'norm_pre_mix': norm_pre_mix, 'w_in': w_in, 'norm_v': norm_v, 'w_s': w_s, 'b_s': b_s,
        'w_out_a': w_out_a, 'lam_re': lam_re, 'lam_im': lam_im, 'log_dt': log_dt,
        'b_re': b_re, 'b_im': b_im, 'c_re': c_re, 'c_im': c_im, 'd_skip': d_skip,
        'w_glu': w_glu, 'w_o': w_o, 'norm_post_mix': norm_post_mix,
        'norm_pre_ff': norm_pre_ff, 'w_ff1': w_ff1, 'w_ff2': w_ff2,
        'norm_post_ff': norm_post_ff,
    }
    y_prompt = trunk(x_prompt, p)
    y_sample = trunk(x_sample, p)
    return (y_prompt, y_sample)
```

```python
import functools

import jax
import jax.numpy as jnp
import numpy as np
from jax import lax
from jax.experimental import pallas as pl
from jax.experimental.pallas import tpu as pltpu

D_MODEL = 1024
GMLP_CHUNK = 128
A_GROUPS = 8
GROUPS = 64
GROUP_DIM = 16
STATE = 64
D_FF = 4 * D_MODEL
EPS = 1e-6

LANES = 128
SUBLANES = 8
CHUNK_L = 8
NSEG = SUBLANES
NBLK = D_MODEL // LANES
GPB = LANES // GROUP_DIM
BLK_STATE = GPB * STATE
SEG_PAD = SUBLANES
SUB_ROWS = 256
FF_ROWS = 512
VMEM_LIMIT = 56 * 1024 * 1024

_F32 = jnp.float32
_BF16 = jnp.bfloat16


def _dot(a, b):
    return jnp.dot(a, b, preferred_element_type=_F32)


def _rms(x, g):
    return x * lax.rsqrt(jnp.mean(x * x, axis=-1, keepdims=True) + EPS) * g


def _sigmoid(x):
    return 1.0 / (1.0 + jnp.exp(-x))


def _gelu(x):
    return jax.nn.gelu(x, approximate=True)


def _const_spec(shape):
    nd = len(shape)
    return pl.BlockSpec(shape, lambda *_: (0,) * nd, pipeline_mode=pl.Buffered(1))


def _params(n_axes):
    return pltpu.CompilerParams(
        dimension_semantics=("arbitrary",) * n_axes, vmem_limit_bytes=VMEM_LIMIT)


def _cmul(ar, ai, br, bi):
    return ar * br - ai * bi, ar * bi + ai * br


def _disc(lre, lim, ldt):
    dt = jnp.exp(ldt)
    mag = jnp.exp(lre * dt)
    return mag * jnp.cos(lim * dt), mag * jnp.sin(lim * dt)


def _wprep_kernel(lre_ref, lim_ref, ldt_ref, bre_ref, bim_ref, cre_ref, cim_ref,
                  lre2_ref, lim2_ref, ldt2_ref,
                  ere_ref, eim_ref, fre_ref, fnim_ref, pwre_ref, pwim_ref, *, nk):
    lre, lim = lre_ref[0], lim_ref[0]
    ar, ai = _disc(lre, lim, ldt_ref[0])
    den = lre * lre + lim * lim
    qr = ((ar - 1.0) * lre + ai * lim) / den
    qi = (ai * lre - (ar - 1.0) * lim) / den
    er, ei = _cmul(qr, qi, bre_ref[0], bim_ref[0])
    for k in range(CHUNK_L):
        ere_ref[0, k] = er
        eim_ref[0, k] = ei
        er, ei = _cmul(ar, ai, er, ei)
    fr, fi = cre_ref[0], cim_ref[0]
    for k in range(CHUNK_L + 1):
        fre_ref[0, k] = fr
        fnim_ref[0, k] = -fi
        fr, fi = _cmul(ar, ai, fr, fi)
    a2r, a2i = _disc(lre2_ref[0], lim2_ref[0], ldt2_ref[0])
    alr, ali = a2r, a2i
    for _ in range(CHUNK_L - 1):
        alr, ali = _cmul(alr, ali, a2r, a2i)
    pwre_ref[0, 0] = jnp.ones_like(alr)
    pwim_ref[0, 0] = jnp.zeros_like(alr)

    def body(k, carry):
        pr, pi = _cmul(carry[0], carry[1], alr, ali)
        pwre_ref[0, k + 1] = pr
        pwim_ref[0, k + 1] = pi
        return pr, pi

    lax.fori_loop(0, nk, body, (jnp.ones_like(alr), jnp.zeros_like(alr)))


def _wprep(lam_re, lam_im, log_dt, b_re, b_im, c_re, c_im, nk):
    n = lam_re.shape[0]
    hp = GROUP_DIM * STATE
    tile = lambda v: jnp.tile(v, (1, 1, GROUP_DIM))
    ldt_w = jnp.broadcast_to(log_dt[:, :, None], (n, GROUPS, hp))
    ldt_c = jnp.broadcast_to(log_dt[:, :, None], (n, GROUPS, STATE))
    bt = lambda v: jnp.transpose(v, (0, 1, 3, 2)).reshape(n, GROUPS, hp)
    ct = lambda v: v.reshape(n, GROUPS, hp)
    wide = pl.BlockSpec((1, GROUPS, hp), lambda i: (i, 0, 0))
    compact = pl.BlockSpec((1, GROUPS, STATE), lambda i: (i, 0, 0))
    e_spec = pl.BlockSpec((1, CHUNK_L, GROUPS, hp), lambda i: (i, 0, 0, 0))
    f_spec = pl.BlockSpec((1, CHUNK_L + 1, GROUPS, hp), lambda i: (i, 0, 0, 0))
    p_spec = pl.BlockSpec((1, nk + 1, GROUPS, STATE), lambda i: (i, 0, 0, 0))
    e_shape = jax.ShapeDtypeStruct((n, CHUNK_L, GROUPS, hp), _F32)
    f_shape = jax.ShapeDtypeStruct((n, CHUNK_L + 1, GROUPS, hp), _F32)
    p_shape = jax.ShapeDtypeStruct((n, nk + 1, GROUPS, STATE), _F32)
    return pl.pallas_call(
        functools.partial(_wprep_kernel, nk=nk),
        grid=(n,),
        in_specs=[wide] * 7 + [compact] * 3,
        out_specs=[e_spec, e_spec, f_spec, f_spec, p_spec, p_spec],
        out_shape=[e_shape, e_shape, f_shape, f_shape, p_shape, p_shape],
        compiler_params=_params(1),
        name="s5_weight_prep",
    )(tile(lam_re), tile(lam_im), ldt_w, bt(b_re), bt(b_im), ct(c_re), ct(c_im),
      lam_re, lam_im, ldt_c)


def _lagk_kernel(fre_ref, fnim_ref, bbr_ref, bbi_ref, kf_ref, kb_ref):
    def lag_kernels(d):
        def ein(x, y):
            return jnp.einsum("gmk,gnk->gmn", x, y, precision=lax.Precision.HIGHEST,
                              preferred_element_type=_F32)
        return ein(fre_ref[0, d], bbr_ref[0, d]) + ein(fnim_ref[0, d], bbi_ref[0, d])

    k_fwd, k_bwd = lag_kernels(0), lag_kernels(1)
    row = lax.broadcasted_iota(jnp.int32, k_fwd.shape, 1)
    kf_ref[0] = k_fwd + jnp.where(row < GROUP_DIM, k_bwd, 0.0)
    kb_ref[0] = k_bwd


def _lagk(f_re, f_nim, bb_re, bb_im):
    depth = f_re.shape[0]
    gb = 8
    lh = CHUNK_L * GROUP_DIM
    f_spec = pl.BlockSpec((1, 2, gb, lh, STATE), lambda l, g: (l, 0, g, 0, 0))
    b_spec = pl.BlockSpec((1, 2, gb, GROUP_DIM, STATE), lambda l, g: (l, 0, g, 0, 0))
    o_spec = pl.BlockSpec((1, gb, lh, GROUP_DIM), lambda l, g: (l, g, 0, 0))
    o_shape = jax.ShapeDtypeStruct((depth, GROUPS, lh, GROUP_DIM), _F32)
    return pl.pallas_call(
        _lagk_kernel,
        grid=(depth, GROUPS // gb),
        in_specs=[f_spec, f_spec, b_spec, b_spec],
        out_specs=[o_spec, o_spec],
        out_shape=[o_shape, o_shape],
        compiler_params=_params(2),
        name="s5_lag_kernels",
    )(f_re, f_nim, bb_re, bb_im)


def _s5_matrices(lam_re, lam_im, log_dt, b_re, b_im, c_re, c_im, d_skip, nk):
    depth = lam_re.shape[0]
    n = depth * 2
    flat = lambda v: v.reshape((n,) + v.shape[2:])
    e_re, e_im, f_re, f_nim, pw_re, pw_im = _wprep(
        flat(lam_re), flat(lam_im), flat(log_dt), flat(b_re), flat(b_im),
        flat(c_re), flat(c_im), nk)
    L, G, H, P = CHUNK_L, GROUPS, GROUP_DIM, STATE
    unflat = lambda v: v.reshape((depth, 2) + v.shape[1:])
    e_re, e_im = unflat(e_re).reshape(depth, 2, L, G, H, P), unflat(e_im).reshape(depth, 2, L, G, H, P)
    f_re = unflat(f_re).reshape(depth, 2, L + 1, G, H, P)
    f_nim = unflat(f_nim).reshape(depth, 2, L + 1, G, H, P)
    pw_re, pw_im = unflat(pw_re), unflat(pw_im)

    to_rows = lambda f: jnp.transpose(f[:, :, :L], (0, 1, 3, 2, 4, 5)).reshape(depth, 2, G, L * H, P)
    kf, kb = _lagk(to_rows(f_re), to_rows(f_nim), e_re[:, :, 0], e_im[:, :, 0])
    kf = kf.reshape(depth, G, L, H, H)
    kb = kb.reshape(depth, G, L, H, H)
    tau = np.arange(L)[:, None]
    t = np.arange(L)[None, :]
    lag_f = np.clip(t - tau, 0, L - 1)
    lag_b = np.clip(tau - t, 0, L - 1)
    causal = jnp.asarray(t >= tau)[None, None, :, :, None, None]
    kc = jnp.where(causal, kf[:, :, lag_f], kb[:, :, lag_b])
    eye = jnp.eye(GPB, dtype=_F32)
    kc = kc.reshape(depth, NBLK, GPB, L, L, H, H)
    t_mat = jnp.einsum("lbgstkh,gj->lbsghtjk", kc, eye)
    t_mat = t_mat.reshape(depth, NBLK, L * LANES, L * LANES).astype(_BF16)

    def wi(e):
        return jnp.stack([e[:, 0, ::-1], e[:, 1]], axis=1)
    wi_c = jnp.stack([wi(e_re), wi(e_im)], axis=2)
    wi_c = wi_c.reshape(depth, 2, 2, L, NBLK, GPB, H, P)
    w_in = jnp.einsum("ldcsbghp,gj->lbsghdcjp", wi_c, eye)
    w_in = w_in.reshape(depth, NBLK, L * LANES, 4 * BLK_STATE).astype(_BF16)

    def wo(f):
        return jnp.stack([f[:, 0, 1:], f[:, 1, 1:][:, ::-1]], axis=1)
    wo_c = jnp.stack([wo(f_re), wo(f_nim)], axis=2)
    wo_c = wo_c.reshape(depth, 2, 2, L, NBLK, GPB, H, P)
    w_out = jnp.einsum("ldctbghp,gj->lbdcgptjh", wo_c, eye)
    w_out = w_out.reshape(depth, NBLK, 4 * BLK_STATE, L * LANES).astype(_BF16)

    pw = jnp.stack([pw_re, pw_im], axis=2)
    pw = pw.reshape(depth, 2, 2, nk + 1, NBLK, BLK_STATE)
    pw = jnp.transpose(pw, (0, 4, 1, 2, 3, 5))

    skip = jnp.tile(d_skip.reshape(depth, NBLK, 1, LANES), (1, 1, 1, L))
    return t_mat, w_in, w_out, pw, skip


def _inproj_kernel(x_ref, npm_ref, win_ref, nv_ref, ws_ref, bs_ref, woa_ref,
                   ga_ref, u_ref, gb_ref, mix_ref, *, seg):
    d = D_MODEL

    def sub(i, _):
        r0 = pl.multiple_of(i * SUB_ROWS, SUB_ROWS)
        rows = pl.ds(r0, SUB_ROWS)
        h = _rms(x_ref[rows, :], npm_ref[...]).astype(_BF16)
        ua = _gelu(_dot(h, win_ref[:, 0:d]))
        va = _gelu(_dot(h, win_ref[:, d:2 * d]))
        vn = _rms(va, nv_ref[...]).astype(_BF16)
        for n in range(SUB_ROWS // GMLP_CHUNK):
            rs = slice(n * GMLP_CHUNK, (n + 1) * GMLP_CHUNK)
            for g in range(A_GROUPS):
                cs = slice(g * LANES, (g + 1) * LANES)
                mix_ref[rs, cs] = _dot(ws_ref[g], vn[rs, cs]) + bs_ref[:, cs]
        z = (ua * mix_ref[...]).astype(_BF16)
        a = _dot(z, woa_ref[...])
        u_ref[rows, :] = _dot(h, win_ref[:, 2 * d:3 * d])
        ga_ref[rows, :] = (_sigmoid(_dot(h, win_ref[:, 3 * d:4 * d])) * a).astype(_BF16)
        gb_ref[rows, :] = _sigmoid(_dot(h, win_ref[:, 4 * d:5 * d])).astype(_BF16)
        return 0

    lax.fori_loop(0, seg // SUB_ROWS, sub, 0)
    u_ref[seg:seg + SEG_PAD, :] = jnp.zeros((SEG_PAD, d), _F32)


def _inproj(x, npm, w_in, nv, w_s, bias, w_out_a, seg):
    ntok = x.shape[0]
    nt = ntok // seg
    pitch = seg + SEG_PAD
    tok = pl.BlockSpec((seg, D_MODEL), lambda i: (i, 0))
    return pl.pallas_call(
        functools.partial(_inproj_kernel, seg=seg),
        grid=(nt,),
        in_specs=[tok, _const_spec((1, D_MODEL)), _const_spec(w_in.shape),
                  _const_spec((1, D_MODEL)), _const_spec(w_s.shape),
                  _const_spec(bias.shape), _const_spec(w_out_a.shape)],
        out_specs=[tok, pl.BlockSpec((pitch, D_MODEL), lambda i: (i, 0)), tok],
        out_shape=[jax.ShapeDtypeStruct((ntok, D_MODEL), _BF16),
                   jax.ShapeDtypeStruct((nt * pitch, D_MODEL), _F32),
                   jax.ShapeDtypeStruct((ntok, D_MODEL), _BF16)],
        scratch_shapes=[pltpu.VMEM((SUB_ROWS, D_MODEL), _F32)],
        compiler_params=_params(1),
        name="inproj_gmlp",
    )(x, npm, w_in, nv, w_s, bias, w_out_a)


def _s5_kernel(u_ref, t_ref, wi_ref, wo_ref, pw_ref, skip_ref, y_ref,
               lhs_ref, z_ref, s_ref, yp_ref, *, nk, pitch):
    nl = CHUNK_L * LANES
    ns = BLK_STATE

    def gather(k, _):
        r = pl.multiple_of(k * SUBLANES, SUBLANES)
        for tau in range(CHUNK_L):
            lhs_ref[pl.ds(r, SUBLANES), tau * LANES:(tau + 1) * LANES] = (
                u_ref[pl.ds(k * CHUNK_L + tau, NSEG, stride=pitch), :])
        return 0

    lax.fori_loop(0, nk, gather, 0)
    lhs = lhs_ref[...].astype(_BF16)

    seg_id = lax.broadcasted_iota(jnp.int32, (NSEG, ns), 0)
    for d in range(2):
        backward = d == 1
        z_ref[...] = _dot(lhs, wi_ref[0, :, d * 2 * ns:(d + 1) * 2 * ns])
        al_r = jnp.broadcast_to(pw_ref[0, d, 0, 1:2, :], (NSEG, ns))
        al_i = jnp.broadcast_to(pw_ref[0, d, 1, 1:2, :], (NSEG, ns))

        def local_scan(j, carry):
            k = nk - 1 - j if backward else j
            r = pl.ds(pl.multiple_of(k * SUBLANES, SUBLANES), SUBLANES)
            sr, si = carry
            zr, zi = z_ref[r, 0:ns], z_ref[r, ns:2 * ns]
            z_ref[r, 0:ns] = sr
            z_ref[r, ns:2 * ns] = si
            pr, pi = _cmul(al_r, al_i, sr, si)
            return pr + zr, pi + zi

        zero = jnp.zeros((NSEG, ns), _F32)
        tot_r, tot_i = lax.fori_loop(0, nk, local_scan, (zero, zero))

        as_r = jnp.broadcast_to(pw_ref[0, d, 0, nk:nk + 1, :], (NSEG, ns))
        as_i = jnp.broadcast_to(pw_ref[0, d, 1, nk:nk + 1, :], (NSEG, ns))
        edge = NSEG - 1 if backward else 0
        shift = NSEG - 1 if backward else 1
        cr, ci = zero, zero
        for _ in range(NSEG - 1):
            pr, pi = _cmul(as_r, as_i, cr, ci)
            cr = jnp.where(seg_id == edge, 0.0, pltpu.roll(pr + tot_r, shift, 0))
            ci = jnp.where(seg_id == edge, 0.0, pltpu.roll(pi + tot_i, shift, 0))

        def add_carry(k, _):
            r = pl.ds(pl.multiple_of(k * SUBLANES, SUBLANES), SUBLANES)
            e = nk - 1 - k if backward else k
            wr = pw_ref[0, d, 0, pl.ds(e, 1), :]
            wi = pw_ref[0, d, 1, pl.ds(e, 1), :]
            pr, pi = _cmul(wr, wi, cr, ci)
            z_ref[r, 0:ns] = z_ref[r, 0:ns] + pr
            z_ref[r, ns:2 * ns] = z_ref[r, ns:2 * ns] + pi
            return 0

        lax.fori_loop(0, nk, add_carry, 0)
        s_ref[:, d * 2 * ns:(d + 1) * 2 * ns] = z_ref[...].astype(_BF16)

    yp_ref[...] = (_dot(lhs, t_ref[0]) + _dot(s_ref[...], wo_ref[0])
                   + skip_ref[0] * lhs_ref[...])

    def scatter(k, _):
        r = pl.multiple_of(k * SUBLANES, SUBLANES)
        for t in range(CHUNK_L):
            y_ref[pl.ds(k * CHUNK_L + t, NSEG, stride=pitch), :] = (
                yp_ref[pl.ds(r, SUBLANES), t * LANES:(t + 1) * LANES])
        return 0

    lax.fori_loop(0, nk, scatter, 0)
    seg = nk * CHUNK_L
    for s in range(NSEG):
        y_ref[s * pitch + seg:(s + 1) * pitch, :] = jnp.zeros((SEG_PAD, LANES), _F32)


def _s5_core(u_pad, t_mat, w_in, w_out, pw, skip, nseq, seg):
    nk = seg // CHUNK_L
    pitch = seg + SEG_PAD
    m = nk * NSEG
    nl = CHUNK_L * LANES
    ns4 = 4 * BLK_STATE
    seq_blk = pl.BlockSpec((NSEG * pitch, LANES), lambda b, s: (s, b))

    def wspec(shape):
        nd = len(shape)
        return pl.BlockSpec((1,) + shape[1:], lambda b, s: (b,) + (0,) * (nd - 1),
                            pipeline_mode=pl.Buffered(1))

    return pl.pallas_call(
        functools.partial(_s5_kernel, nk=nk, pitch=pitch),
        grid=(NBLK, nseq),
        in_specs=[seq_blk, wspec(t_mat.shape), wspec(w_in.shape), wspec(w_out.shape),
                  wspec(pw.shape), wspec(skip.shape)],
        out_specs=seq_blk,
        out_shape=jax.ShapeDtypeStruct(u_pad.shape, _F32),
        scratch_shapes=[pltpu.VMEM((m, nl), _F32),
                        pltpu.VMEM((m, 2 * BLK_STATE), _F32),
                        pltpu.VMEM((m, ns4), _BF16),
                        pltpu.VMEM((m, nl), _F32)],
        compiler_params=_params(2),
        name="s5_core",
    )(u_pad, t_mat, w_in, w_out, pw, skip)


def _merge_kernel(y_ref, ga_ref, gb_ref, x_ref, wglu_ref, wo_ref, npost_ref, o_ref, *, seg):
    d = D_MODEL

    def sub(i, _):
        rows = pl.ds(pl.multiple_of(i * SUB_ROWS, SUB_ROWS), SUB_ROWS)
        z = _gelu(y_ref[rows, :]).astype(_BF16)
        b = _dot(z, wglu_ref[:, 0:d]) * _sigmoid(_dot(z, wglu_ref[:, d:2 * d]))
        m = (ga_ref[rows, :].astype(_F32) + gb_ref[rows, :].astype(_F32) * b).astype(_BF16)
        o_ref[rows, :] = x_ref[rows, :] + _rms(_dot(m, wo_ref[...]), npost_ref[...])
        return 0

    lax.fori_loop(0, seg // SUB_ROWS, sub, 0)


def _merge(y_pad, ga, gb, x, w_glu, w_o, npost, seg):
    ntok = x.shape[0]
    pitch = seg + SEG_PAD
    tok = pl.BlockSpec((seg, D_MODEL), lambda i: (i, 0))
    return pl.pallas_call(
        functools.partial(_merge_kernel, seg=seg),
        grid=(ntok // seg,),
        in_specs=[pl.BlockSpec((pitch, D_MODEL), lambda i: (i, 0)), tok, tok, tok,
                  _const_spec(w_glu.shape), _const_spec(w_o.shape), _const_spec((1, D_MODEL))],
        out_specs=tok,
        out_shape=jax.ShapeDtypeStruct((ntok, D_MODEL), _F32),
        compiler_params=_params(1),
        name="glu_merge",
    )(y_pad, ga, gb, x, w_glu, w_o, npost)


def _ffn_kernel(x_ref, npre_ref, w1_ref, w2_ref, npost_ref, o_ref):
    x = x_ref[...]
    h = _rms(x, npre_ref[...]).astype(_BF16)
    f = jnp.zeros(x.shape, _F32)
    for j in range(D_FF // D_MODEL):
        cs = slice(j * D_MODEL, (j + 1) * D_MODEL)
        r = jnp.maximum(_dot(h, w1_ref[:, cs]), 0.0)
        f = f + _dot((r * r).astype(_BF16), w2_ref[cs, :])
    o_ref[...] = x + _rms(f, npost_ref[...])


def _ffn(x, npre, w1, w2, npost, rows):
    ntok = x.shape[0]
    tok = pl.BlockSpec((rows, D_MODEL), lambda i: (i, 0))
    return pl.pallas_call(
        _ffn_kernel,
        grid=(ntok // rows,),
        in_specs=[tok, _const_spec((1, D_MODEL)), _const_spec(w1.shape),
                  _const_spec(w2.shape), _const_spec((1, D_MODEL))],
        out_specs=tok,
        out_shape=jax.ShapeDtypeStruct((ntok, D_MODEL), _F32),
        compiler_params=_params(1),
        name="ffn",
    )(x, npre, w1, w2, npost)


def _trunk(x, p, s5m):
    nseq, seqlen, d = x.shape
    seg = seqlen // NSEG
    assert d == D_MODEL and seg % SUB_ROWS == 0 and seqlen % FF_ROWS == 0
    x = x.reshape(nseq * seqlen, d)
    t_mat, w_si, w_so, pw, skip = s5m
    row = lambda v: v.reshape(1, D_MODEL)
    for l in range(p["w_in"].shape[0]):
        bias = jnp.repeat(p["b_s"][l].T, LANES, axis=1)
        ga, u_pad, gb = _inproj(x, row(p["norm_pre_mix"][l]), p["w_in"][l], row(p["norm_v"][l]),
                                p["w_s"][l], bias, p["w_out_a"][l], seg)
        y_pad = _s5_core(u_pad, t_mat[l], w_si[l], w_so[l], pw[l], skip[l], nseq, seg)
        x = _merge(y_pad, ga, gb, x, p["w_glu"][l], p["w_o"][l], row(p["norm_post_mix"][l]), seg)
        x = _ffn(x, row(p["norm_pre_ff"][l]), p["w_ff1"][l], p["w_ff2"][l],
                 row(p["norm_post_ff"][l]), FF_ROWS)
    return x.reshape(nseq, seqlen, d)


def kernel(x_prompt, x_sample, norm_pre_mix, w_in, norm_v, w_s, b_s, w_out_a, lam_re, lam_im,
           log_dt, b_re, b_im, c_re, c_im, d_skip, w_glu, w_o, norm_post_mix, norm_pre_ff,
           w_ff1, w_ff2, norm_post_ff):
    bf = lambda w: w.astype(_BF16)
    p = {
        "norm_pre_mix": norm_pre_mix, "w_in": bf(w_in), "norm_v": norm_v, "w_s": bf(w_s),
        "b_s": b_s, "w_out_a": bf(w_out_a), "w_glu": bf(w_glu), "w_o": bf(w_o),
        "norm_post_mix": norm_post_mix, "norm_pre_ff": norm_pre_ff, "w_ff1": bf(w_ff1),
        "w_ff2": bf(w_ff2), "norm_post_ff": norm_post_ff,
    }
    outs = []
    mats = {}
    for x in (x_prompt, x_sample):
        nk = x.shape[1] // (NSEG * CHUNK_L)
        if nk not in mats:
            mats[nk] = _s5_matrices(lam_re, lam_im, log_dt, b_re, b_im, c_re, c_im, d_skip, nk)
        outs.append(_trunk(x, p, mats[nk]))
    return tuple(outs)
```

```python
import functools

import jax
import jax.numpy as jnp
from jax import lax
from jax.experimental import pallas as pl
from jax.experimental.pallas import tpu as pltpu

D_MODEL = 1024
GMLP_CHUNK = 128
A_GROUPS = 8
GROUPS = 64
GROUP_DIM = 16
STATE = 64
D_FF = 4 * D_MODEL
EPS = 1e-6

LANES = 128
SUBLANES = 8
CHUNK_L = 16
NSEG = SUBLANES
NBLK = D_MODEL // LANES
PAIR_LANES = 2 * GROUP_DIM
PAIRS_PER_BLK = LANES // PAIR_LANES
NPAIR = GROUPS // 2
PAIR_K = CHUNK_L * PAIR_LANES
PAIR_STATE = 2 * STATE
BLK_STATE = PAIRS_PER_BLK * PAIR_STATE
SEG_PAD = SUBLANES
ROW_BLK = 4 * SUBLANES
SUB_ROWS = 256
FF_ROWS = 512
VMEM_LIMIT = 56 * 1024 * 1024

_F32 = jnp.float32
_BF16 = jnp.bfloat16


def _dot(a, b):
    return jnp.dot(a, b, preferred_element_type=_F32)


def _rms(x, g):
    return x * lax.rsqrt(jnp.mean(x * x, axis=-1, keepdims=True) + EPS) * g


def _sigmoid(x):
    return 1.0 / (1.0 + jnp.exp(-x))


def _gelu(x):
    return jax.nn.gelu(x, approximate=True)


def _layer_spec(shape, layer):
    nd = len(shape)
    return pl.BlockSpec((1,) + tuple(shape[1:]), lambda *_: (layer,) + (0,) * (nd - 1),
                        pipeline_mode=pl.Buffered(1))


def _params(n_axes):
    return pltpu.CompilerParams(
        dimension_semantics=("arbitrary",) * n_axes, vmem_limit_bytes=VMEM_LIMIT)


def _cmul(ar, ai, br, bi):
    return ar * br - ai * bi, ar * bi + ai * br


def _wprep_kernel(lre_ref, lim_ref, ldt_ref, bre_ref, bim_ref, cre_ref, cim_ref,
                  t_ref, wsi_ref, wso_ref, pw_ref,
                  xr_ref, xi_ref, yr_ref, yni_ref, eo_ref, *, nk):
    L = CHUNK_L
    lane = lax.broadcasted_iota(jnp.int32, (GROUP_DIM, LANES), 1)
    lane1 = lax.broadcasted_iota(jnp.int32, (1, LANES), 1)
    for d in range(2):
        a_pair = []
        for g2 in range(2):
            lre, lim = lre_ref[0, d, g2], lim_ref[0, d, g2]
            dt = jnp.exp(ldt_ref[0, d, g2])
            mag = jnp.exp(lre * dt)
            ar, ai = mag * jnp.cos(lim * dt), mag * jnp.sin(lim * dt)
            a_pair.append((ar, ai))
            den = lre * lre + lim * lim
            qr = ((ar - 1.0) * lre + ai * lim) / den
            qi = (ai * lre - (ar - 1.0) * lim) / den
            n2 = ar * ar + ai * ai
            ir, ii = ar / n2, -ai / n2
            keep = (lane >= g2 * STATE) & (lane < (g2 + 1) * STATE)
            pr, pi = _cmul(qr, qi, jnp.where(keep, bre_ref[0, d, g2], 0.0),
                           jnp.where(keep, bim_ref[0, d, g2], 0.0))
            nr, ni = pr, pi
            fr = jnp.where(keep, cre_ref[0, d, g2], 0.0)
            fi = jnp.where(keep, cim_ref[0, d, g2], 0.0)
            gr, gi = fr, fi

            def rows(k):
                return slice(k * PAIR_LANES + g2 * GROUP_DIM, k * PAIR_LANES + (g2 + 1) * GROUP_DIM)

            for k in range(L + 1):
                if k < L:
                    if d == 0:
                        xr_ref[0, rows(k), :] = nr
                        xi_ref[0, rows(k), :] = ni
                        yr_ref[0, rows(k), :] = fr
                        yni_ref[0, rows(k), :] = -fi
                        wsi_ref[0, 0, rows(L - 1 - k), 0:LANES] = pr.astype(_BF16)
                        wsi_ref[0, 0, rows(L - 1 - k), LANES:2 * LANES] = pi.astype(_BF16)
                    else:
                        xr_ref[1, rows(k), :] = pr
                        xi_ref[1, rows(k), :] = pi
                        yr_ref[1, rows(k), :] = gr
                        yni_ref[1, rows(k), :] = -gi
                        wsi_ref[0, 0, rows(k), 2 * LANES:3 * LANES] = pr.astype(_BF16)
                        wsi_ref[0, 0, rows(k), 3 * LANES:4 * LANES] = pi.astype(_BF16)
                if k >= 1:
                    t = k - 1 if d == 0 else L - k
                    eo_ref[2 * d, rows(t), :] = fr
                    eo_ref[2 * d + 1, rows(t), :] = -fi
                pr, pi = _cmul(ar, ai, pr, pi)
                nr, ni = _cmul(ir, ii, nr, ni)
                fr, fi = _cmul(ar, ai, fr, fi)
                gr, gi = _cmul(ir, ii, gr, gi)

        a_r = jnp.where(lane1 < STATE, a_pair[0][0], a_pair[1][0])
        a_i = jnp.where(lane1 < STATE, a_pair[0][1], a_pair[1][1])
        alr, ali = a_r, a_i
        for _ in range(L - 1):
            alr, ali = _cmul(alr, ali, a_r, a_i)
        one, zero = jnp.ones((1, LANES), _F32), jnp.zeros((1, LANES), _F32)
        pw_ref[0, d, 0, 0:1, :] = one
        pw_ref[0, d, 1, 0:1, :] = zero

        def power(k, carry):
            wr, wi = _cmul(carry[0], carry[1], alr, ali)
            pw_ref[0, d, 0, pl.ds(k + 1, 1), :] = wr
            pw_ref[0, d, 1, pl.ds(k + 1, 1), :] = wi
            return wr, wi

        lax.fori_loop(0, nk, power, (one, zero))

    def lag_matrix(d):
        def nt(x, y):
            return lax.dot_general(x, y, (((1,), (1,)), ((), ())),
                                   precision=lax.Precision.HIGHEST, preferred_element_type=_F32)
        return nt(jnp.concatenate([xr_ref[d], xi_ref[d]], axis=1),
                  jnp.concatenate([yr_ref[d], yni_ref[d]], axis=1))

    shift = PAIR_LANES.bit_length() - 1
    tau = lax.broadcasted_iota(jnp.int32, (PAIR_K, PAIR_K), 0) >> shift
    t = lax.broadcasted_iota(jnp.int32, (PAIR_K, PAIR_K), 1) >> shift
    t_ref[0, 0] = (jnp.where(t >= tau, lag_matrix(0), 0.0)
                   + jnp.where(tau >= t, lag_matrix(1), 0.0)).astype(_BF16)
    for j in range(4):
        wso_ref[0, 0, j * LANES:(j + 1) * LANES, :] = eo_ref[j].T.astype(_BF16)


def _s5_matrices(lam_re, lam_im, log_dt, b_re, b_im, c_re, c_im, nk):
    depth = lam_re.shape[0]
    dup = lambda v: jnp.concatenate([v, v], axis=-1)
    row = lambda v: dup(v)[:, :, :, None, :]
    ldt = jnp.broadcast_to(log_dt[..., None, None], (depth, 2, GROUPS, 1, LANES))
    bt = lambda v: dup(jnp.swapaxes(v, -1, -2))
    vec = pl.BlockSpec((1, 2, 2, 1, LANES), lambda l, q: (l, 0, q, 0, 0))
    mat = pl.BlockSpec((1, 2, 2, GROUP_DIM, LANES), lambda l, q: (l, 0, q, 0, 0))
    sq = pl.BlockSpec((1, 1, PAIR_K, PAIR_K), lambda l, q: (l, q, 0, 0))
    sq_shape = jax.ShapeDtypeStruct((depth, NPAIR, PAIR_K, PAIR_K), _BF16)
    return pl.pallas_call(
        functools.partial(_wprep_kernel, nk=nk),
        grid=(depth, NPAIR),
        in_specs=[vec, vec, vec, mat, mat, mat, mat],
        out_specs=[sq, sq, sq,
                   pl.BlockSpec((1, 2, 2, nk + 1, LANES), lambda l, q: (l, 0, 0, 0, q))],
        out_shape=[sq_shape, sq_shape, sq_shape,
                   jax.ShapeDtypeStruct((depth, 2, 2, nk + 1, NPAIR * LANES), _F32)],
        scratch_shapes=[pltpu.VMEM((2, PAIR_K, LANES), _F32)] * 4
                       + [pltpu.VMEM((4, PAIR_K, LANES), _F32)],
        compiler_params=_params(2),
        name="s5_weight_prep",
    )(row(lam_re), row(lam_im), ldt, bt(b_re), bt(b_im), dup(c_re), dup(c_im))


def _inproj_kernel(x_ref, npm_ref, win_ref, nv_ref, ws_ref, bs_ref, woa_ref,
                   ga_ref, u_ref, gb_ref, mix_ref, *, seg):
    d = D_MODEL

    def sub(i, _):
        r0 = pl.multiple_of(i * SUB_ROWS, SUB_ROWS)
        rows = pl.ds(r0, SUB_ROWS)
        h = _rms(x_ref[rows, :], npm_ref[0]).astype(_BF16)
        ua = _gelu(_dot(h, win_ref[0, :, 0:d]))
        va = _gelu(_dot(h, win_ref[0, :, d:2 * d]))
        vn = _rms(va, nv_ref[0]).astype(_BF16)
        for n in range(SUB_ROWS // GMLP_CHUNK):
            rs = slice(n * GMLP_CHUNK, (n + 1) * GMLP_CHUNK)
            for g in range(A_GROUPS):
                cs = slice(g * LANES, (g + 1) * LANES)
                mix_ref[rs, cs] = _dot(ws_ref[0, g], vn[rs, cs]) + bs_ref[0, :, cs]
        z = (ua * mix_ref[...]).astype(_BF16)
        a = _dot(z, woa_ref[0])
        u_ref[rows, :] = _dot(h, win_ref[0, :, 2 * d:3 * d])
        ga_ref[rows, :] = (_sigmoid(_dot(h, win_ref[0, :, 3 * d:4 * d])) * a).astype(_BF16)
        gb_ref[rows, :] = _sigmoid(_dot(h, win_ref[0, :, 4 * d:5 * d])).astype(_BF16)
        return 0

    lax.fori_loop(0, seg // SUB_ROWS, sub, 0)
    u_ref[seg:seg + SEG_PAD, :] = jnp.zeros((SEG_PAD, d), _F32)


def _inproj(x, npm, w_in, nv, w_s, bias, w_out_a, layer, seg):
    ntok = x.shape[0]
    nt = ntok // seg
    pitch = seg + SEG_PAD
    tok = pl.BlockSpec((seg, D_MODEL), lambda i: (i, 0))
    return pl.pallas_call(
        functools.partial(_inproj_kernel, seg=seg),
        grid=(nt,),
        in_specs=[tok] + [_layer_spec(w.shape, layer) for w in (npm, w_in, nv, w_s, bias, w_out_a)],
        out_specs=[tok, pl.BlockSpec((pitch, D_MODEL), lambda i: (i, 0)), tok],
        out_shape=[jax.ShapeDtypeStruct((ntok, D_MODEL), _BF16),
                   jax.ShapeDtypeStruct((nt * pitch, D_MODEL), _F32),
                   jax.ShapeDtypeStruct((ntok, D_MODEL), _BF16)],
        scratch_shapes=[pltpu.VMEM((SUB_ROWS, D_MODEL), _F32)],
        compiler_params=_params(1),
        name="inproj_gmlp",
    )(x, npm, w_in, nv, w_s, bias, w_out_a)


def _granule_transpose(a):
    lane = lax.broadcasted_iota(jnp.int32, a[0].shape, 1)
    half = lane < 2 * PAIR_LANES
    even = (lane & (2 * PAIR_LANES - 1)) < PAIR_LANES
    far = 2 * PAIR_LANES
    c0 = jnp.where(half, a[0], pltpu.roll(a[2], far, 1))
    c2 = jnp.where(half, pltpu.roll(a[0], far, 1), a[2])
    c1 = jnp.where(half, a[1], pltpu.roll(a[3], far, 1))
    c3 = jnp.where(half, pltpu.roll(a[1], far, 1), a[3])
    up, down = PAIR_LANES, LANES - PAIR_LANES
    return [jnp.where(even, c0, pltpu.roll(c1, up, 1)), jnp.where(even, pltpu.roll(c0, down, 1), c1),
            jnp.where(even, c2, pltpu.roll(c3, up, 1)), jnp.where(even, pltpu.roll(c2, down, 1), c3)]


def _s5_kernel(u_ref, t_ref, wi_ref, wo_ref, pw_ref, skip_ref, y_ref,
               lhs_ref, z_ref, s_ref, yp_ref, *, nk, pitch):
    ns = BLK_STATE
    kpb = ROW_BLK // SUBLANES
    tgroups = CHUNK_L // PAIRS_PER_BLK

    def seg_rows(k, tau):
        return pl.ds(k * CHUNK_L + tau, NSEG, stride=pitch)

    def gather(i, _):
        r = pl.ds(pl.multiple_of(i * ROW_BLK, ROW_BLK), ROW_BLK)
        for tg in range(tgroups):
            a = [jnp.concatenate([u_ref[seg_rows(i * kpb + kk, tg * 4 + tl), :] for kk in range(kpb)], axis=0)
                 for tl in range(4)]
            b = _granule_transpose(a)
            for j in range(PAIRS_PER_BLK):
                lhs_ref[j, r, tg * LANES:(tg + 1) * LANES] = b[j].astype(_BF16)
        return 0

    lax.fori_loop(0, nk // kpb, gather, 0, unroll=2)

    seg_id = lax.broadcasted_iota(jnp.int32, (NSEG, ns), 0)
    for d in range(2):
        backward = d == 1
        for j in range(PAIRS_PER_BLK):
            zz = _dot(lhs_ref[j], wi_ref[0, j, :, d * 2 * LANES:(d + 1) * 2 * LANES])
            z_ref[:, j * LANES:(j + 1) * LANES] = zz[:, 0:LANES]
            z_ref[:, ns + j * LANES:ns + (j + 1) * LANES] = zz[:, LANES:2 * LANES]
        al_r = jnp.broadcast_to(pw_ref[0, d, 0, 1:2, :], (NSEG, ns))
        al_i = jnp.broadcast_to(pw_ref[0, d, 1, 1:2, :], (NSEG, ns))

        def local_scan(j, carry):
            k = nk - 1 - j if backward else j
            r = pl.ds(pl.multiple_of(k * SUBLANES, SUBLANES), SUBLANES)
            sr, si = carry
            zr, zi = z_ref[r, 0:ns], z_ref[r, ns:2 * ns]
            z_ref[r, 0:ns] = sr
            z_ref[r, ns:2 * ns] = si
            pr, pi = _cmul(al_r, al_i, sr, si)
            return pr + zr, pi + zi

        zero = jnp.zeros((NSEG, ns), _F32)
        tot_r, tot_i = lax.fori_loop(0, nk, local_scan, (zero, zero))

        as_r = jnp.broadcast_to(pw_ref[0, d, 0, nk:nk + 1, :], (NSEG, ns))
        as_i = jnp.broadcast_to(pw_ref[0, d, 1, nk:nk + 1, :], (NSEG, ns))
        edge = NSEG - 1 if backward else 0
        shift = NSEG - 1 if backward else 1
        cr, ci = zero, zero
        for _ in range(NSEG - 1):
            pr, pi = _cmul(as_r, as_i, cr, ci)
            cr = jnp.where(seg_id == edge, 0.0, pltpu.roll(pr + tot_r, shift, 0))
            ci = jnp.where(seg_id == edge, 0.0, pltpu.roll(pi + tot_i, shift, 0))

        def add_carry(k, _):
            r = pl.ds(pl.multiple_of(k * SUBLANES, SUBLANES), SUBLANES)
            e = nk - 1 - k if backward else k
            wr = pw_ref[0, d, 0, pl.ds(e, 1), :]
            wi = pw_ref[0, d, 1, pl.ds(e, 1), :]
            pr, pi = _cmul(wr, wi, cr, ci)
            z_ref[r, 0:ns] = z_ref[r, 0:ns] + pr
            z_ref[r, ns:2 * ns] = z_ref[r, ns:2 * ns] + pi
            return 0

        lax.fori_loop(0, nk, add_carry, 0)
        s_ref[:, d * 2 * ns:(d + 1) * 2 * ns] = z_ref[...].astype(_BF16)

    for j in range(PAIRS_PER_BLK):
        sp = jnp.concatenate([s_ref[:, q * ns + j * LANES:q * ns + (j + 1) * LANES] for q in range(4)],
                             axis=1)
        yp_ref[j] = _dot(lhs_ref[j], t_ref[0, j]) + _dot(sp, wo_ref[0, j])

    skip = skip_ref[0, 0]

    def scatter(i, _):
        r = pl.ds(pl.multiple_of(i * ROW_BLK, ROW_BLK), ROW_BLK)
        for tg in range(tgroups):
            b = _granule_transpose([yp_ref[j, r, tg * LANES:(tg + 1) * LANES] for j in range(PAIRS_PER_BLK)])
            for tl in range(4):
                for kk in range(kpb):
                    rows = seg_rows(i * kpb + kk, tg * 4 + tl)
                    y_ref[rows, :] = b[tl][kk * SUBLANES:(kk + 1) * SUBLANES] + skip * u_ref[rows, :]
        return 0

    lax.fori_loop(0, nk // kpb, scatter, 0, unroll=2)
    seg = nk * CHUNK_L
    for s in range(NSEG):
        y_ref[s * pitch + seg:(s + 1) * pitch, :] = jnp.zeros((SEG_PAD, LANES), _F32)


def _s5_core(u_pad, t_mat, w_si, w_so, pw, skip, layer, nseq, seg):
    nk = seg // CHUNK_L
    pitch = seg + SEG_PAD
    m = nk * NSEG
    seq_blk = pl.BlockSpec((NSEG * pitch, LANES), lambda b, s: (s, b))
    sq = pl.BlockSpec((1, PAIRS_PER_BLK, PAIR_K, PAIR_K), lambda b, s: (layer, b, 0, 0))
    return pl.pallas_call(
        functools.partial(_s5_kernel, nk=nk, pitch=pitch),
        grid=(NBLK, nseq),
        in_specs=[seq_blk, sq, sq, sq,
                  pl.BlockSpec((1, 2, 2, nk + 1, BLK_STATE), lambda b, s: (layer, 0, 0, 0, b)),
                  pl.BlockSpec((1, 1, 1, LANES), lambda b, s: (layer, b, 0, 0))],
        out_specs=seq_blk,
        out_shape=jax.ShapeDtypeStruct(u_pad.shape, _F32),
        scratch_shapes=[pltpu.VMEM((PAIRS_PER_BLK, m, PAIR_K), _BF16),
                        pltpu.VMEM((m, 2 * BLK_STATE), _F32),
                        pltpu.VMEM((m, 4 * BLK_STATE), _BF16),
                        pltpu.VMEM((PAIRS_PER_BLK, m, PAIR_K), _F32)],
        compiler_params=_params(2),
        name="s5_core",
    )(u_pad, t_mat, w_si, w_so, pw, skip)


def _merge_kernel(y_ref, ga_ref, gb_ref, x_ref, wglu_ref, wo_ref, npost_ref, o_ref, *, seg):
    d = D_MODEL

    def sub(i, _):
        rows = pl.ds(pl.multiple_of(i * SUB_ROWS, SUB_ROWS), SUB_ROWS)
        z = _gelu(y_ref[rows, :]).astype(_BF16)
        b = _dot(z, wglu_ref[0, :, 0:d]) * _sigmoid(_dot(z, wglu_ref[0, :, d:2 * d]))
        m = (ga_ref[rows, :].astype(_F32) + gb_ref[rows, :].astype(_F32) * b).astype(_BF16)
        o_ref[rows, :] = x_ref[rows, :] + _rms(_dot(m, wo_ref[0]), npost_ref[0])
        return 0

    lax.fori_loop(0, seg // SUB_ROWS, sub, 0)


def _merge(y_pad, ga, gb, x, w_glu, w_o, npost, layer, seg):
    ntok = x.shape[0]
    pitch = seg + SEG_PAD
    tok = pl.BlockSpec((seg, D_MODEL), lambda i: (i, 0))
    return pl.pallas_call(
        functools.partial(_merge_kernel, seg=seg),
        grid=(ntok // seg,),
        in_specs=[pl.BlockSpec((pitch, D_MODEL), lambda i: (i, 0)), tok, tok, tok]
                 + [_layer_spec(w.shape, layer) for w in (w_glu, w_o, npost)],
        out_specs=tok,
        out_shape=jax.ShapeDtypeStruct((ntok, D_MODEL), _F32),
        compiler_params=_params(1),
        name="glu_merge",
    )(y_pad, ga, gb, x, w_glu, w_o, npost)


def _ffn_kernel(x_ref, npre_ref, w1_ref, w2_ref, npost_ref, o_ref):
    x = x_ref[...]
    h = _rms(x, npre_ref[0]).astype(_BF16)
    f = jnp.zeros(x.shape, _F32)
    for j in range(D_FF // D_MODEL):
        cs = slice(j * D_MODEL, (j + 1) * D_MODEL)
        r = jnp.maximum(_dot(h, w1_ref[0, :, cs]), 0.0)
        f = f + _dot((r * r).astype(_BF16), w2_ref[0, cs, :])
    o_ref[...] = x + _rms(f, npost_ref[0])


def _ffn(x, npre, w1, w2, npost, layer, rows):
    ntok = x.shape[0]
    tok = pl.BlockSpec((rows, D_MODEL), lambda i: (i, 0))
    return pl.pallas_call(
        _ffn_kernel,
        grid=(ntok // rows,),
        in_specs=[tok] + [_layer_spec(w.shape, layer) for w in (npre, w1, w2, npost)],
        out_specs=tok,
        out_shape=jax.ShapeDtypeStruct((ntok, D_MODEL), _F32),
        compiler_params=_params(1),
        name="ffn",
    )(x, npre, w1, w2, npost)


def _trunk(x, p, s5m):
    nseq, seqlen, d = x.shape
    seg = seqlen // NSEG
    assert d == D_MODEL and seg % SUB_ROWS == 0 and seqlen % FF_ROWS == 0
    x = x.reshape(nseq * seqlen, d)
    t_mat, w_si, w_so, pw = s5m
    for l in range(p["w_in"].shape[0]):
        ga, u_pad, gb = _inproj(x, p["norm_pre_mix"], p["w_in"], p["norm_v"], p["w_s"], p["bias"],
                                p["w_out_a"], l, seg)
        y_pad = _s5_core(u_pad, t_mat, w_si, w_so, pw, p["d_skip"], l, nseq, seg)
        x = _merge(y_pad, ga, gb, x, p["w_glu"], p["w_o"], p["norm_post_mix"], l, seg)
        x = _ffn(x, p["norm_pre_ff"], p["w_ff1"], p["w_ff2"], p["norm_post_ff"], l, FF_ROWS)
    return x.reshape(nseq, seqlen, d)


def kernel(x_prompt, x_sample, norm_pre_mix, w_in, norm_v, w_s, b_s, w_out_a, lam_re, lam_im,
           log_dt, b_re, b_im, c_re, c_im, d_skip, w_glu, w_o, norm_post_mix, norm_pre_ff,
           w_ff1, w_ff2, norm_post_ff):
    depth = w_in.shape[0]
    bf = lambda w: w.astype(_BF16)
    row = lambda v: v.reshape(depth, 1, D_MODEL)
    p = {
        "norm_pre_mix": row(norm_pre_mix), "w_in": bf(w_in), "norm_v": row(norm_v), "w_s": bf(w_s),
        "bias": jnp.repeat(jnp.swapaxes(b_s, 1, 2), LANES, axis=2),
        "w_out_a": bf(w_out_a), "d_skip": d_skip.reshape(depth, NBLK, 1, LANES),
        "w_glu": bf(w_glu), "w_o": bf(w_o), "norm_post_mix": row(norm_post_mix),
        "norm_pre_ff": row(norm_pre_ff), "w_ff1": bf(w_ff1), "w_ff2": bf(w_ff2),
        "norm_post_ff": row(norm_post_ff),
    }
    outs = []
    mats = {}
    for x in (x_prompt, x_sample):
        nk = x.shape[1] // (NSEG * CHUNK_L)
        if nk not in mats:
            mats[nk] = _s5_matrices(lam_re, lam_im, log_dt, b_re, b_im, c_re, c_im, nk)
        outs.append(_trunk(x, p, mats[nk]))
    return tuple(outs)
```

```python
import functools

import jax
import jax.numpy as jnp
from jax import lax
from jax.experimental import pallas as pl
from jax.experimental.pallas import tpu as pltpu

D_MODEL = 1024
GMLP_CHUNK = 128
A_GROUPS = 8
GROUPS = 64
GROUP_DIM = 16
STATE = 64
D_FF = 4 * D_MODEL
EPS = 1e-6

LANES = 128
SUBLANES = 8
CHUNK_L = 16
NSEG = SUBLANES
NBLK = D_MODEL // LANES
PAIR_LANES = 2 * GROUP_DIM
PAIRS_PER_BLK = LANES // PAIR_LANES
NPAIR = GROUPS // 2
PAIR_K = CHUNK_L * PAIR_LANES
PAIR_STATE = 2 * STATE
BLK_STATE = PAIRS_PER_BLK * PAIR_STATE
SEG_PAD = SUBLANES
ROW_BLK = 2 * SUBLANES
KPB = ROW_BLK // SUBLANES
SUB_ROWS = 256
FF_ROWS = 512
VMEM_LIMIT = 56 * 1024 * 1024

_F32 = jnp.float32
_BF16 = jnp.bfloat16


def _dot(a, b):
    return jnp.dot(a, b, preferred_element_type=_F32)


def _rms(x, g):
    return x * lax.rsqrt(jnp.mean(x * x, axis=-1, keepdims=True) + EPS) * g


def _sigmoid(x):
    return 1.0 / (1.0 + jnp.exp(-x))


def _gelu(x):
    return jax.nn.gelu(x, approximate=True)


def _layer_spec(shape, layer):
    nd = len(shape)
    return pl.BlockSpec((1,) + tuple(shape[1:]), lambda *_: (layer,) + (0,) * (nd - 1),
                        pipeline_mode=pl.Buffered(1))


def _params(n_axes):
    return pltpu.CompilerParams(
        dimension_semantics=("arbitrary",) * n_axes, vmem_limit_bytes=VMEM_LIMIT)


def _cmul(ar, ai, br, bi):
    return ar * br - ai * bi, ar * bi + ai * br


def _wprep_kernel(lre_ref, lim_ref, ldt_ref, bre_ref, bim_ref, cre_ref, cim_ref,
                  t_ref, wsi_ref, wso_ref, pw_ref,
                  xr_ref, xi_ref, yr_ref, yni_ref, eo_ref, *, nk):
    L = CHUNK_L
    lane = lax.broadcasted_iota(jnp.int32, (GROUP_DIM, LANES), 1)
    lane1 = lax.broadcasted_iota(jnp.int32, (1, LANES), 1)
    for d in range(2):
        a_pair = []
        for g2 in range(2):
            lre, lim = lre_ref[0, d, g2], lim_ref[0, d, g2]
            dt = jnp.exp(ldt_ref[0, d, g2])
            mag = jnp.exp(lre * dt)
            ar, ai = mag * jnp.cos(lim * dt), mag * jnp.sin(lim * dt)
            a_pair.append((ar, ai))
            den = lre * lre + lim * lim
            qr = ((ar - 1.0) * lre + ai * lim) / den
            qi = (ai * lre - (ar - 1.0) * lim) / den
            n2 = ar * ar + ai * ai
            ir, ii = ar / n2, -ai / n2
            keep = (lane >= g2 * STATE) & (lane < (g2 + 1) * STATE)
            pr, pi = _cmul(qr, qi, jnp.where(keep, bre_ref[0, d, g2], 0.0),
                           jnp.where(keep, bim_ref[0, d, g2], 0.0))
            nr, ni = pr, pi
            fr = jnp.where(keep, cre_ref[0, d, g2], 0.0)
            fi = jnp.where(keep, cim_ref[0, d, g2], 0.0)
            gr, gi = fr, fi

            def rows(k):
                return slice(k * PAIR_LANES + g2 * GROUP_DIM, k * PAIR_LANES + (g2 + 1) * GROUP_DIM)

            for k in range(L + 1):
                if k < L:
                    if d == 0:
                        xr_ref[0, rows(k), :] = nr
                        xi_ref[0, rows(k), :] = ni
                        yr_ref[0, rows(k), :] = fr
                        yni_ref[0, rows(k), :] = -fi
                        wsi_ref[0, 0, rows(L - 1 - k), 0:LANES] = pr.astype(_BF16)
                        wsi_ref[0, 0, rows(L - 1 - k), LANES:2 * LANES] = pi.astype(_BF16)
                    else:
                        xr_ref[1, rows(k), :] = pr
                        xi_ref[1, rows(k), :] = pi
                        yr_ref[1, rows(k), :] = gr
                        yni_ref[1, rows(k), :] = -gi
                        wsi_ref[0, 0, rows(k), 2 * LANES:3 * LANES] = pr.astype(_BF16)
                        wsi_ref[0, 0, rows(k), 3 * LANES:4 * LANES] = pi.astype(_BF16)
                if k >= 1:
                    t = k - 1 if d == 0 else L - k
                    eo_ref[2 * d, rows(t), :] = fr
                    eo_ref[2 * d + 1, rows(t), :] = -fi
                pr, pi = _cmul(ar, ai, pr, pi)
                nr, ni = _cmul(ir, ii, nr, ni)
                fr, fi = _cmul(ar, ai, fr, fi)
                gr, gi = _cmul(ir, ii, gr, gi)

        a_r = jnp.where(lane1 < STATE, a_pair[0][0], a_pair[1][0])
        a_i = jnp.where(lane1 < STATE, a_pair[0][1], a_pair[1][1])
        alr, ali = a_r, a_i
        for _ in range(L - 1):
            alr, ali = _cmul(alr, ali, a_r, a_i)
        one, zero = jnp.ones((1, LANES), _F32), jnp.zeros((1, LANES), _F32)
        pw_ref[0, d, 0, 0:1, :] = one
        pw_ref[0, d, 1, 0:1, :] = zero

        def power(k, carry):
            wr, wi = _cmul(carry[0], carry[1], alr, ali)
            pw_ref[0, d, 0, pl.ds(k + 1, 1), :] = wr
            pw_ref[0, d, 1, pl.ds(k + 1, 1), :] = wi
            return wr, wi

        lax.fori_loop(0, nk, power, (one, zero))

    def lag_matrix(d):
        def nt(x, y):
            return lax.dot_general(x, y, (((1,), (1,)), ((), ())),
                                   precision=lax.Precision.HIGHEST, preferred_element_type=_F32)
        return nt(jnp.concatenate([xr_ref[d], xi_ref[d]], axis=1),
                  jnp.concatenate([yr_ref[d], yni_ref[d]], axis=1))

    shift = PAIR_LANES.bit_length() - 1
    tau = lax.broadcasted_iota(jnp.int32, (PAIR_K, PAIR_K), 0) >> shift
    t = lax.broadcasted_iota(jnp.int32, (PAIR_K, PAIR_K), 1) >> shift
    t_ref[0, 0] = (jnp.where(t >= tau, lag_matrix(0), 0.0)
                   + jnp.where(tau >= t, lag_matrix(1), 0.0)).astype(_BF16)
    for j in range(4):
        wso_ref[0, 0, j * LANES:(j + 1) * LANES, :] = eo_ref[j].T.astype(_BF16)


def _s5_matrices(lam_re, lam_im, log_dt, b_re, b_im, c_re, c_im, nk):
    depth = lam_re.shape[0]
    dup = lambda v: jnp.concatenate([v, v], axis=-1)
    row = lambda v: dup(v)[:, :, :, None, :]
    ldt = jnp.broadcast_to(log_dt[..., None, None], (depth, 2, GROUPS, 1, LANES))
    bt = lambda v: dup(jnp.swapaxes(v, -1, -2))
    vec = pl.BlockSpec((1, 2, 2, 1, LANES), lambda l, q: (l, 0, q, 0, 0))
    mat = pl.BlockSpec((1, 2, 2, GROUP_DIM, LANES), lambda l, q: (l, 0, q, 0, 0))
    sq = pl.BlockSpec((1, 1, PAIR_K, PAIR_K), lambda l, q: (l, q, 0, 0))
    sq_shape = jax.ShapeDtypeStruct((depth, NPAIR, PAIR_K, PAIR_K), _BF16)
    return pl.pallas_call(
        functools.partial(_wprep_kernel, nk=nk),
        grid=(depth, NPAIR),
        in_specs=[vec, vec, vec, mat, mat, mat, mat],
        out_specs=[sq, sq, sq,
                   pl.BlockSpec((1, 2, 2, nk + 1, LANES), lambda l, q: (l, 0, 0, 0, q))],
        out_shape=[sq_shape, sq_shape, sq_shape,
                   jax.ShapeDtypeStruct((depth, 2, 2, nk + 1, NPAIR * LANES), _F32)],
        scratch_shapes=[pltpu.VMEM((2, PAIR_K, LANES), _F32)] * 4
                       + [pltpu.VMEM((4, PAIR_K, LANES), _F32)],
        compiler_params=_params(2),
        name="s5_weight_prep",
    )(row(lam_re), row(lam_im), ldt, bt(b_re), bt(b_im), dup(c_re), dup(c_im))


def _inproj_kernel(x_ref, npm_ref, win_ref, nv_ref, ws_ref, bs_ref, woa_ref,
                   ga_ref, u_ref, gb_ref, mix_ref, *, seg):
    d = D_MODEL

    def sub(i, _):
        r0 = pl.multiple_of(i * SUB_ROWS, SUB_ROWS)
        rows = pl.ds(r0, SUB_ROWS)
        h = _rms(x_ref[rows, :], npm_ref[0]).astype(_BF16)
        ua = _gelu(_dot(h, win_ref[0, :, 0:d]))
        va = _gelu(_dot(h, win_ref[0, :, d:2 * d]))
        vn = _rms(va, nv_ref[0]).astype(_BF16)
        for n in range(SUB_ROWS // GMLP_CHUNK):
            rs = slice(n * GMLP_CHUNK, (n + 1) * GMLP_CHUNK)
            for g in range(A_GROUPS):
                cs = slice(g * LANES, (g + 1) * LANES)
                mix_ref[rs, cs] = _dot(ws_ref[0, g], vn[rs, cs]) + bs_ref[0, :, cs]
        z = (ua * mix_ref[...]).astype(_BF16)
        a = _dot(z, woa_ref[0])
        u_ref[rows, :] = _dot(h, win_ref[0, :, 2 * d:3 * d])
        ga_ref[rows, :] = (_sigmoid(_dot(h, win_ref[0, :, 3 * d:4 * d])) * a).astype(_BF16)
        gb_ref[rows, :] = _sigmoid(_dot(h, win_ref[0, :, 4 * d:5 * d])).astype(_BF16)
        return 0

    lax.fori_loop(0, seg // SUB_ROWS, sub, 0)
    u_ref[seg:seg + SEG_PAD, :] = jnp.zeros((SEG_PAD, d), _F32)


def _inproj(x, npm, w_in, nv, w_s, bias, w_out_a, layer, seg):
    ntok = x.shape[0]
    nt = ntok // seg
    pitch = seg + SEG_PAD
    tok = pl.BlockSpec((seg, D_MODEL), lambda i: (i, 0))
    return pl.pallas_call(
        functools.partial(_inproj_kernel, seg=seg),
        grid=(nt,),
        in_specs=[tok] + [_layer_spec(w.shape, layer) for w in (npm, w_in, nv, w_s, bias, w_out_a)],
        out_specs=[tok, pl.BlockSpec((pitch, D_MODEL), lambda i: (i, 0)), tok],
        out_shape=[jax.ShapeDtypeStruct((ntok, D_MODEL), _BF16),
                   jax.ShapeDtypeStruct((nt * pitch, D_MODEL), _F32),
                   jax.ShapeDtypeStruct((ntok, D_MODEL), _BF16)],
        scratch_shapes=[pltpu.VMEM((SUB_ROWS, D_MODEL), _F32)],
        compiler_params=_params(1),
        name="inproj_gmlp",
    )(x, npm, w_in, nv, w_s, bias, w_out_a)


def _s5_kernel(u_ref, t_ref, wi_ref, wo_ref, pw_ref, skip_ref, perm_ref, y_ref,
               raw_ref, lhs_ref, z_ref, s_ref, yp_ref, yf_ref, *, nk, pitch):
    ns = BLK_STATE
    tgroups = CHUNK_L // PAIRS_PER_BLK
    tg_cols = PAIRS_PER_BLK * LANES

    def seg_rows(k, tau):
        return pl.ds(k * CHUNK_L + tau, NSEG, stride=pitch)

    def gather(i, _):
        r = pl.ds(pl.multiple_of(i * ROW_BLK, ROW_BLK), ROW_BLK)
        for tau in range(CHUNK_L):
            v = jnp.concatenate([u_ref[seg_rows(i * KPB + kk, tau), :] for kk in range(KPB)], axis=0)
            raw_ref[r, tau * LANES:(tau + 1) * LANES] = v.astype(_BF16)
        return 0

    lax.fori_loop(0, nk // KPB, gather, 0, unroll=2)

    for tg in range(tgroups):
        pb = _dot(raw_ref[:, tg * tg_cols:(tg + 1) * tg_cols], perm_ref[...])
        for j in range(PAIRS_PER_BLK):
            lhs_ref[j, :, tg * LANES:(tg + 1) * LANES] = pb[:, j * LANES:(j + 1) * LANES].astype(_BF16)

    seg_id = lax.broadcasted_iota(jnp.int32, (NSEG, ns), 0)
    for d in range(2):
        backward = d == 1
        for j in range(PAIRS_PER_BLK):
            zz = _dot(lhs_ref[j], wi_ref[0, j, :, d * 2 * LANES:(d + 1) * 2 * LANES])
            z_ref[:, j * LANES:(j + 1) * LANES] = zz[:, 0:LANES]
            z_ref[:, ns + j * LANES:ns + (j + 1) * LANES] = zz[:, LANES:2 * LANES]
        al_r = jnp.broadcast_to(pw_ref[0, d, 0, 1:2, :], (NSEG, ns))
        al_i = jnp.broadcast_to(pw_ref[0, d, 1, 1:2, :], (NSEG, ns))

        def local_scan(j, carry):
            k = nk - 1 - j if backward else j
            r = pl.ds(pl.multiple_of(k * SUBLANES, SUBLANES), SUBLANES)
            sr, si = carry
            zr, zi = z_ref[r, 0:ns], z_ref[r, ns:2 * ns]
            z_ref[r, 0:ns] = sr
            z_ref[r, ns:2 * ns] = si
            pr, pi = _cmul(al_r, al_i, sr, si)
            return pr + zr, pi + zi

        zero = jnp.zeros((NSEG, ns), _F32)
        tot_r, tot_i = lax.fori_loop(0, nk, local_scan, (zero, zero))

        as_r = jnp.broadcast_to(pw_ref[0, d, 0, nk:nk + 1, :], (NSEG, ns))
        as_i = jnp.broadcast_to(pw_ref[0, d, 1, nk:nk + 1, :], (NSEG, ns))
        edge = NSEG - 1 if backward else 0
        shift = NSEG - 1 if backward else 1
        cr, ci = zero, zero
        for _ in range(NSEG - 1):
            pr, pi = _cmul(as_r, as_i, cr, ci)
            cr = jnp.where(seg_id == edge, 0.0, pltpu.roll(pr + tot_r, shift, 0))
            ci = jnp.where(seg_id == edge, 0.0, pltpu.roll(pi + tot_i, shift, 0))

        def add_carry(i, _):
            r = pl.ds(pl.multiple_of(i * ROW_BLK, ROW_BLK), ROW_BLK)
            sr, si = [], []
            for kk in range(KPB):
                k = i * KPB + kk
                e = nk - 1 - k if backward else k
                pr, pi = _cmul(pw_ref[0, d, 0, pl.ds(e, 1), :], pw_ref[0, d, 1, pl.ds(e, 1), :], cr, ci)
                rk = pl.ds(pl.multiple_of(k * SUBLANES, SUBLANES), SUBLANES)
                sr.append(z_ref[rk, 0:ns] + pr)
                si.append(z_ref[rk, ns:2 * ns] + pi)
            s_ref[r, d * 2 * ns:d * 2 * ns + ns] = jnp.concatenate(sr, axis=0).astype(_BF16)
            s_ref[r, d * 2 * ns + ns:(d + 1) * 2 * ns] = jnp.concatenate(si, axis=0).astype(_BF16)
            return 0

        lax.fori_loop(0, nk // KPB, add_carry, 0)

    for j in range(PAIRS_PER_BLK):
        sp = jnp.concatenate([s_ref[:, q * ns + j * LANES:q * ns + (j + 1) * LANES] for q in range(4)],
                             axis=1)
        yp_ref[j] = _dot(lhs_ref[j], t_ref[0, j]) + _dot(sp, wo_ref[0, j])

    for tg in range(tgroups):
        yc = jnp.concatenate([yp_ref[j, :, tg * LANES:(tg + 1) * LANES] for j in range(PAIRS_PER_BLK)],
                             axis=1)
        hi = yc.astype(_BF16)
        lo = (yc - hi.astype(_F32)).astype(_BF16)
        yf_ref[:, tg * tg_cols:(tg + 1) * tg_cols] = _dot(hi, perm_ref[...]) + _dot(lo, perm_ref[...])

    skip = skip_ref[0, 0]

    def scatter(k, _):
        r = pl.ds(pl.multiple_of(k * SUBLANES, SUBLANES), SUBLANES)
        for t in range(CHUNK_L):
            rows = seg_rows(k, t)
            y_ref[rows, :] = yf_ref[r, t * LANES:(t + 1) * LANES] + skip * u_ref[rows, :]
        return 0

    lax.fori_loop(0, nk, scatter, 0, unroll=2)
    seg = nk * CHUNK_L
    for s in range(NSEG):
        y_ref[s * pitch + seg:(s + 1) * pitch, :] = jnp.zeros((SEG_PAD, LANES), _F32)


def _pair_permutation():
    n = PAIRS_PER_BLK
    src = jnp.arange(n * LANES)
    tl, j, c = src // LANES, (src // PAIR_LANES) % n, src % PAIR_LANES
    dst = j * LANES + tl * PAIR_LANES + c
    return (dst[:, None] == jnp.arange(n * LANES)[None, :]).astype(_BF16)


def _s5_core(u_pad, t_mat, w_si, w_so, pw, skip, layer, nseq, seg):
    nk = seg // CHUNK_L
    pitch = seg + SEG_PAD
    m = nk * NSEG
    perm = _pair_permutation()
    seq_blk = pl.BlockSpec((NSEG * pitch, LANES), lambda b, s: (s, b))
    sq = pl.BlockSpec((1, PAIRS_PER_BLK, PAIR_K, PAIR_K), lambda b, s: (layer, b, 0, 0))
    return pl.pallas_call(
        functools.partial(_s5_kernel, nk=nk, pitch=pitch),
        grid=(NBLK, nseq),
        in_specs=[seq_blk, sq, sq, sq,
                  pl.BlockSpec((1, 2, 2, nk + 1, BLK_STATE), lambda b, s: (layer, 0, 0, 0, b)),
                  pl.BlockSpec((1, 1, 1, LANES), lambda b, s: (layer, b, 0, 0)),
                  pl.BlockSpec(perm.shape, lambda b, s: (0, 0))],
        out_specs=seq_blk,
        out_shape=jax.ShapeDtypeStruct(u_pad.shape, _F32),
        scratch_shapes=[pltpu.VMEM((m, CHUNK_L * LANES), _BF16),
                        pltpu.VMEM((PAIRS_PER_BLK, m, PAIR_K), _BF16),
                        pltpu.VMEM((m, 2 * BLK_STATE), _F32),
                        pltpu.VMEM((m, 4 * BLK_STATE), _BF16),
                        pltpu.VMEM((PAIRS_PER_BLK, m, PAIR_K), _F32),
                        pltpu.VMEM((m, CHUNK_L * LANES), _F32)],
        compiler_params=_params(2),
        name="s5_core",
    )(u_pad, t_mat, w_si, w_so, pw, skip, perm)


def _merge_kernel(y_ref, ga_ref, gb_ref, x_ref, wglu_ref, wo_ref, npost_ref, o_ref, *, seg):
    d = D_MODEL

    def sub(i, _):
        rows = pl.ds(pl.multiple_of(i * SUB_ROWS, SUB_ROWS), SUB_ROWS)
        z = _gelu(y_ref[rows, :]).astype(_BF16)
        b = _dot(z, wglu_ref[0, :, 0:d]) * _sigmoid(_dot(z, wglu_ref[0, :, d:2 * d]))
        m = (ga_ref[rows, :].astype(_F32) + gb_ref[rows, :].astype(_F32) * b).astype(_BF16)
        o_ref[rows, :] = x_ref[rows, :] + _rms(_dot(m, wo_ref[0]), npost_ref[0])
        return 0

    lax.fori_loop(0, seg // SUB_ROWS, sub, 0)


def _merge(y_pad, ga, gb, x, w_glu, w_o, npost, layer, seg):
    ntok = x.shape[0]
    pitch = seg + SEG_PAD
    tok = pl.BlockSpec((seg, D_MODEL), lambda i: (i, 0))
    return pl.pallas_call(
        functools.partial(_merge_kernel, seg=seg),
        grid=(ntok // seg,),
        in_specs=[pl.BlockSpec((pitch, D_MODEL), lambda i: (i, 0)), tok, tok, tok]
                 + [_layer_spec(w.shape, layer) for w in (w_glu, w_o, npost)],
        out_specs=tok,
        out_shape=jax.ShapeDtypeStruct((ntok, D_MODEL), _F32),
        compiler_params=_params(1),
        name="glu_merge",
    )(y_pad, ga, gb, x, w_glu, w_o, npost)


def _ffn_kernel(x_ref, npre_ref, w1_ref, w2_ref, npost_ref, o_ref):
    x = x_ref[...]
    h = _rms(x, npre_ref[0]).astype(_BF16)
    f = jnp.zeros(x.shape, _F32)
    for j in range(D_FF // D_MODEL):
        cs = slice(j * D_MODEL, (j + 1) * D_MODEL)
        r = jnp.maximum(_dot(h, w1_ref[0, :, cs]), 0.0)
        f = f + _dot((r * r).astype(_BF16), w2_ref[0, cs, :])
    o_ref[...] = x + _rms(f, npost_ref[0])


def _ffn(x, npre, w1, w2, npost, layer, rows):
    ntok = x.shape[0]
    tok = pl.BlockSpec((rows, D_MODEL), lambda i: (i, 0))
    return pl.pallas_call(
        _ffn_kernel,
        grid=(ntok // rows,),
        in_specs=[tok] + [_layer_spec(w.shape, layer) for w in (npre, w1, w2, npost)],
        out_specs=tok,
        out_shape=jax.ShapeDtypeStruct((ntok, D_MODEL), _F32),
        compiler_params=_params(1),
        name="ffn",
    )(x, npre, w1, w2, npost)


def _trunk(x, p, s5m):
    nseq, seqlen, d = x.shape
    seg = seqlen // NSEG
    assert d == D_MODEL and seg % SUB_ROWS == 0 and seqlen % FF_ROWS == 0
    x = x.reshape(nseq * seqlen, d)
    t_mat, w_si, w_so, pw = s5m
    for l in range(p["w_in"].shape[0]):
        ga, u_pad, gb = _inproj(x, p["norm_pre_mix"], p["w_in"], p["norm_v"], p["w_s"], p["bias"],
                                p["w_out_a"], l, seg)
        y_pad = _s5_core(u_pad, t_mat, w_si, w_so, pw, p["d_skip"], l, nseq, seg)
        x = _merge(y_pad, ga, gb, x, p["w_glu"], p["w_o"], p["norm_post_mix"], l, seg)
        x = _ffn(x, p["norm_pre_ff"], p["w_ff1"], p["w_ff2"], p["norm_post_ff"], l, FF_ROWS)
    return x.reshape(nseq, seqlen, d)


def kernel(x_prompt, x_sample, norm_pre_mix, w_in, norm_v, w_s, b_s, w_out_a, lam_re, lam_im,
           log_dt, b_re, b_im, c_re, c_im, d_skip, w_glu, w_o, norm_post_mix, norm_pre_ff,
           w_ff1, w_ff2, norm_post_ff):
    depth = w_in.shape[0]
    bf = lambda w: w.astype(_BF16)
    row = lambda v: v.reshape(depth, 1, D_MODEL)
    p = {
        "norm_pre_mix": row(norm_pre_mix), "w_in": bf(w_in), "norm_v": row(norm_v), "w_s": bf(w_s),
        "bias": jnp.repeat(jnp.swapaxes(b_s, 1, 2), LANES, axis=2),
        "w_out_a": bf(w_out_a), "d_skip": d_skip.reshape(depth, NBLK, 1, LANES),
        "w_glu": bf(w_glu), "w_o": bf(w_o), "norm_post_mix": row(norm_post_mix),
        "norm_pre_ff": row(norm_pre_ff), "w_ff1": bf(w_ff1), "w_ff2": bf(w_ff2),
        "norm_post_ff": row(norm_post_ff),
    }
    outs = []
    mats = {}
    for x in (x_prompt, x_sample):
        nk = x.shape[1] // (NSEG * CHUNK_L)
        if nk not in mats:
            mats[nk] = _s5_matrices(lam_re, lam_im, log_dt, b_re, b_im, c_re, c_im, nk)
        outs.append(_trunk(x, p, mats[nk]))
    return tuple(outs)
```

```python
import functools

import jax
import jax.numpy as jnp
from jax import lax
from jax.experimental import pallas as pl
from jax.experimental.pallas import tpu as pltpu

D_MODEL = 1024
GMLP_CHUNK = 128
A_GROUPS = 8
GROUPS = 64
GROUP_DIM = 16
STATE = 64
D_FF = 4 * D_MODEL
EPS = 1e-6

LANES = 128
SUBLANES = 8
CHUNK_L = 16
NSEG = SUBLANES
NBLK = D_MODEL // LANES
PAIR_LANES = 2 * GROUP_DIM
PAIRS_PER_BLK = LANES // PAIR_LANES
NPAIR = GROUPS // 2
PAIR_K = CHUNK_L * PAIR_LANES
PAIR_STATE = 2 * STATE
BLK_STATE = PAIRS_PER_BLK * PAIR_STATE
SEG_PAD = SUBLANES
ROW_BLK = 2 * SUBLANES
KPB = ROW_BLK // SUBLANES
SCAN_UNROLL = True
SUB_UNROLL = 2
SUB_ROWS = 256
FF_ROWS = 512
VMEM_LIMIT = 56 * 1024 * 1024

_F32 = jnp.float32
_BF16 = jnp.bfloat16


def _dot(a, b):
    return jnp.dot(a, b, preferred_element_type=_F32)


def _rms(x, g):
    return x * lax.rsqrt(jnp.mean(x * x, axis=-1, keepdims=True) + EPS) * g


def _sigmoid(x):
    return 1.0 / (1.0 + jnp.exp(-x))


def _gelu(x):
    return jax.nn.gelu(x, approximate=True)


def _layer_spec(shape, layer):
    nd = len(shape)
    return pl.BlockSpec((1,) + tuple(shape[1:]), lambda *_: (layer,) + (0,) * (nd - 1),
                        pipeline_mode=pl.Buffered(1))


def _params(n_axes):
    return pltpu.CompilerParams(
        dimension_semantics=("arbitrary",) * n_axes, vmem_limit_bytes=VMEM_LIMIT)


def _cmul(ar, ai, br, bi):
    return ar * br - ai * bi, ar * bi + ai * br


def _wprep_kernel(lre_ref, lim_ref, ldt_ref, bre_ref, bim_ref, cre_ref, cim_ref,
                  t_ref, wsi_ref, wso_ref, pw_ref,
                  xr_ref, xi_ref, yr_ref, yni_ref, eo_ref, *, nk):
    L = CHUNK_L
    lane = lax.broadcasted_iota(jnp.int32, (GROUP_DIM, LANES), 1)
    lane1 = lax.broadcasted_iota(jnp.int32, (1, LANES), 1)
    for d in range(2):
        a_pair = []
        for g2 in range(2):
            lre, lim = lre_ref[0, d, g2], lim_ref[0, d, g2]
            dt = jnp.exp(ldt_ref[0, d, g2])
            mag = jnp.exp(lre * dt)
            ar, ai = mag * jnp.cos(lim * dt), mag * jnp.sin(lim * dt)
            a_pair.append((ar, ai))
            den = lre * lre + lim * lim
            qr = ((ar - 1.0) * lre + ai * lim) / den
            qi = (ai * lre - (ar - 1.0) * lim) / den
            n2 = ar * ar + ai * ai
            ir, ii = ar / n2, -ai / n2
            keep = (lane >= g2 * STATE) & (lane < (g2 + 1) * STATE)
            pr, pi = _cmul(qr, qi, jnp.where(keep, bre_ref[0, d, g2], 0.0),
                           jnp.where(keep, bim_ref[0, d, g2], 0.0))
            nr, ni = pr, pi
            fr = jnp.where(keep, cre_ref[0, d, g2], 0.0)
            fi = jnp.where(keep, cim_ref[0, d, g2], 0.0)
            gr, gi = fr, fi

            def rows(k):
                return slice(k * PAIR_LANES + g2 * GROUP_DIM, k * PAIR_LANES + (g2 + 1) * GROUP_DIM)

            for k in range(L + 1):
                if k < L:
                    if d == 0:
                        xr_ref[0, rows(k), :] = nr
                        xi_ref[0, rows(k), :] = ni
                        yr_ref[0, rows(k), :] = fr
                        yni_ref[0, rows(k), :] = -fi
                        wsi_ref[0, 0, rows(L - 1 - k), 0:LANES] = pr.astype(_BF16)
                        wsi_ref[0, 0, rows(L - 1 - k), LANES:2 * LANES] = pi.astype(_BF16)
                    else:
                        xr_ref[1, rows(k), :] = pr
                        xi_ref[1, rows(k), :] = pi
                        yr_ref[1, rows(k), :] = gr
                        yni_ref[1, rows(k), :] = -gi
                        wsi_ref[0, 0, rows(k), 2 * LANES:3 * LANES] = pr.astype(_BF16)
                        wsi_ref[0, 0, rows(k), 3 * LANES:4 * LANES] = pi.astype(_BF16)
                if k >= 1:
                    t = k - 1 if d == 0 else L - k
                    eo_ref[2 * d, rows(t), :] = fr
                    eo_ref[2 * d + 1, rows(t), :] = -fi
                pr, pi = _cmul(ar, ai, pr, pi)
                nr, ni = _cmul(ir, ii, nr, ni)
                fr, fi = _cmul(ar, ai, fr, fi)
                gr, gi = _cmul(ir, ii, gr, gi)

        a_r = jnp.where(lane1 < STATE, a_pair[0][0], a_pair[1][0])
        a_i = jnp.where(lane1 < STATE, a_pair[0][1], a_pair[1][1])
        alr, ali = a_r, a_i
        for _ in range(L - 1):
            alr, ali = _cmul(alr, ali, a_r, a_i)
        one, zero = jnp.ones((1, LANES), _F32), jnp.zeros((1, LANES), _F32)
        pw_ref[0, d, 0, 0:1, :] = one
        pw_ref[0, d, 1, 0:1, :] = zero

        def power(k, carry):
            wr, wi = _cmul(carry[0], carry[1], alr, ali)
            pw_ref[0, d, 0, pl.ds(k + 1, 1), :] = wr
            pw_ref[0, d, 1, pl.ds(k + 1, 1), :] = wi
            return wr, wi

        lax.fori_loop(0, nk, power, (one, zero))

    def split(v):
        hi = v.astype(_BF16)
        return hi, (v - hi.astype(_F32)).astype(_BF16)

    def lag_matrix(d):
        xh, xl = split(jnp.concatenate([xr_ref[d], xi_ref[d]], axis=1))
        yh, yl = split(jnp.concatenate([yr_ref[d], yni_ref[d]], axis=1))
        return lax.dot_general(jnp.concatenate([xh, xh, xl], axis=1), jnp.concatenate([yh, yl, yh], axis=1),
                               (((1,), (1,)), ((), ())), preferred_element_type=_F32)

    shift = PAIR_LANES.bit_length() - 1
    tau = lax.broadcasted_iota(jnp.int32, (PAIR_K, PAIR_K), 0) >> shift
    t = lax.broadcasted_iota(jnp.int32, (PAIR_K, PAIR_K), 1) >> shift
    t_ref[0, 0] = (jnp.where(t >= tau, lag_matrix(0), 0.0)
                   + jnp.where(tau >= t, lag_matrix(1), 0.0)).astype(_BF16)
    for j in range(4):
        wso_ref[0, 0, j * LANES:(j + 1) * LANES, :] = eo_ref[j].T.astype(_BF16)


def _s5_matrices(lam_re, lam_im, log_dt, b_re, b_im, c_re, c_im, nk):
    depth = lam_re.shape[0]
    dup = lambda v: jnp.concatenate([v, v], axis=-1)
    row = lambda v: dup(v)[:, :, :, None, :]
    ldt = jnp.broadcast_to(log_dt[..., None, None], (depth, 2, GROUPS, 1, LANES))
    bt = lambda v: dup(jnp.swapaxes(v, -1, -2))
    vec = pl.BlockSpec((1, 2, 2, 1, LANES), lambda l, q: (l, 0, q, 0, 0))
    mat = pl.BlockSpec((1, 2, 2, GROUP_DIM, LANES), lambda l, q: (l, 0, q, 0, 0))
    sq = pl.BlockSpec((1, 1, PAIR_K, PAIR_K), lambda l, q: (l, q, 0, 0))
    sq_shape = jax.ShapeDtypeStruct((depth, NPAIR, PAIR_K, PAIR_K), _BF16)
    return pl.pallas_call(
        functools.partial(_wprep_kernel, nk=nk),
        grid=(depth, NPAIR),
        in_specs=[vec, vec, vec, mat, mat, mat, mat],
        out_specs=[sq, sq, sq,
                   pl.BlockSpec((1, 2, 2, nk + 1, LANES), lambda l, q: (l, 0, 0, 0, q))],
        out_shape=[sq_shape, sq_shape, sq_shape,
                   jax.ShapeDtypeStruct((depth, 2, 2, nk + 1, NPAIR * LANES), _F32)],
        scratch_shapes=[pltpu.VMEM((2, PAIR_K, LANES), _F32)] * 4
                       + [pltpu.VMEM((4, PAIR_K, LANES), _F32)],
        compiler_params=_params(2),
        name="s5_weight_prep",
    )(row(lam_re), row(lam_im), ldt, bt(b_re), bt(b_im), dup(c_re), dup(c_im))


def _inproj_kernel(x_ref, npm_ref, win_ref, nv_ref, ws_ref, bs_ref, woa_ref,
                   ga_ref, u_ref, gb_ref, mix_ref, *, seg):
    d = D_MODEL

    def sub(i, _):
        r0 = pl.multiple_of(i * SUB_ROWS, SUB_ROWS)
        rows = pl.ds(r0, SUB_ROWS)
        h = _rms(x_ref[rows, :], npm_ref[0]).astype(_BF16)
        ua = _gelu(_dot(h, win_ref[0, :, 0:d]))
        va = _gelu(_dot(h, win_ref[0, :, d:2 * d]))
        vn = _rms(va, nv_ref[0]).astype(_BF16)
        for n in range(SUB_ROWS // GMLP_CHUNK):
            rs = slice(n * GMLP_CHUNK, (n + 1) * GMLP_CHUNK)
            for g in range(A_GROUPS):
                cs = slice(g * LANES, (g + 1) * LANES)
                mix_ref[rs, cs] = _dot(ws_ref[0, g], vn[rs, cs]) + bs_ref[0, :, cs]
        z = (ua * mix_ref[...]).astype(_BF16)
        a = _dot(z, woa_ref[0])
        u_ref[rows, :] = _dot(h, win_ref[0, :, 2 * d:3 * d])
        ga_ref[rows, :] = (_sigmoid(_dot(h, win_ref[0, :, 3 * d:4 * d])) * a).astype(_BF16)
        gb_ref[rows, :] = _sigmoid(_dot(h, win_ref[0, :, 4 * d:5 * d])).astype(_BF16)
        return 0

    lax.fori_loop(0, seg // SUB_ROWS, sub, 0, unroll=SUB_UNROLL)
    u_ref[seg:seg + SEG_PAD, :] = jnp.zeros((SEG_PAD, d), _F32)


def _inproj(x, npm, w_in, nv, w_s, bias, w_out_a, layer, seg):
    ntok = x.shape[0]
    nt = ntok // seg
    pitch = seg + SEG_PAD
    tok = pl.BlockSpec((seg, D_MODEL), lambda i: (i, 0))
    return pl.pallas_call(
        functools.partial(_inproj_kernel, seg=seg),
        grid=(nt,),
        in_specs=[tok] + [_layer_spec(w.shape, layer) for w in (npm, w_in, nv, w_s, bias, w_out_a)],
        out_specs=[tok, pl.BlockSpec((pitch, D_MODEL), lambda i: (i, 0)), tok],
        out_shape=[jax.ShapeDtypeStruct((ntok, D_MODEL), _BF16),
                   jax.ShapeDtypeStruct((nt * pitch, D_MODEL), _F32),
                   jax.ShapeDtypeStruct((ntok, D_MODEL), _BF16)],
        scratch_shapes=[pltpu.VMEM((SUB_ROWS, D_MODEL), _F32)],
        compiler_params=_params(1),
        name="inproj_gmlp",
    )(x, npm, w_in, nv, w_s, bias, w_out_a)


def _s5_kernel(u_ref, t_ref, wi_ref, wo_ref, pw_ref, skip_ref, perm_ref, y_ref,
               raw_ref, lhs_ref, z_ref, s_ref, yp_ref, yf_ref, *, nk, pitch):
    ns = BLK_STATE
    tgroups = CHUNK_L // PAIRS_PER_BLK
    tg_cols = PAIRS_PER_BLK * LANES

    def seg_rows(k, tau):
        return pl.ds(k * CHUNK_L + tau, NSEG, stride=pitch)

    def gather(i, _):
        r = pl.ds(pl.multiple_of(i * ROW_BLK, ROW_BLK), ROW_BLK)
        for tau in range(CHUNK_L):
            v = jnp.concatenate([u_ref[seg_rows(i * KPB + kk, tau), :] for kk in range(KPB)], axis=0)
            raw_ref[r, tau * LANES:(tau + 1) * LANES] = v.astype(_BF16)
        return 0

    lax.fori_loop(0, nk // KPB, gather, 0, unroll=2)

    for tg in range(tgroups):
        pb = _dot(raw_ref[:, tg * tg_cols:(tg + 1) * tg_cols], perm_ref[...])
        for j in range(PAIRS_PER_BLK):
            lhs_ref[j, :, tg * LANES:(tg + 1) * LANES] = pb[:, j * LANES:(j + 1) * LANES].astype(_BF16)

    seg_id = lax.broadcasted_iota(jnp.int32, (NSEG, ns), 0)
    for d in range(2):
        for j in range(PAIRS_PER_BLK):
            zz = _dot(lhs_ref[j], wi_ref[0, j, :, d * 2 * LANES:(d + 1) * 2 * LANES])
            z_ref[d, :, j * LANES:(j + 1) * LANES] = zz[:, 0:LANES]
            z_ref[d, :, ns + j * LANES:ns + (j + 1) * LANES] = zz[:, LANES:2 * LANES]
    for j in range(PAIRS_PER_BLK):
        yp_ref[j] = _dot(lhs_ref[j], t_ref[0, j])

    for d in range(2):
        backward = d == 1
        al_r = jnp.broadcast_to(pw_ref[0, d, 0, 1:2, :], (NSEG, ns))
        al_i = jnp.broadcast_to(pw_ref[0, d, 1, 1:2, :], (NSEG, ns))

        def local_scan(j, carry):
            k = nk - 1 - j if backward else j
            r = pl.ds(pl.multiple_of(k * SUBLANES, SUBLANES), SUBLANES)
            sr, si = carry
            zr, zi = z_ref[d, r, 0:ns], z_ref[d, r, ns:2 * ns]
            z_ref[d, r, 0:ns] = sr
            z_ref[d, r, ns:2 * ns] = si
            pr, pi = _cmul(al_r, al_i, sr, si)
            return pr + zr, pi + zi

        zero = jnp.zeros((NSEG, ns), _F32)
        tot_r, tot_i = lax.fori_loop(0, nk, local_scan, (zero, zero), unroll=SCAN_UNROLL)

        as_r = jnp.broadcast_to(pw_ref[0, d, 0, nk:nk + 1, :], (NSEG, ns))
        as_i = jnp.broadcast_to(pw_ref[0, d, 1, nk:nk + 1, :], (NSEG, ns))
        edge = NSEG - 1 if backward else 0
        shift = NSEG - 1 if backward else 1
        cr, ci = zero, zero
        for _ in range(NSEG - 1):
            pr, pi = _cmul(as_r, as_i, cr, ci)
            cr = jnp.where(seg_id == edge, 0.0, pltpu.roll(pr + tot_r, shift, 0))
            ci = jnp.where(seg_id == edge, 0.0, pltpu.roll(pi + tot_i, shift, 0))

        def add_carry(i, _):
            r = pl.ds(pl.multiple_of(i * ROW_BLK, ROW_BLK), ROW_BLK)
            sr, si = [], []
            for kk in range(KPB):
                k = i * KPB + kk
                e = nk - 1 - k if backward else k
                pr, pi = _cmul(pw_ref[0, d, 0, pl.ds(e, 1), :], pw_ref[0, d, 1, pl.ds(e, 1), :], cr, ci)
                rk = pl.ds(pl.multiple_of(k * SUBLANES, SUBLANES), SUBLANES)
                sr.append(z_ref[d, rk, 0:ns] + pr)
                si.append(z_ref[d, rk, ns:2 * ns] + pi)
            s_ref[r, d * 2 * ns:d * 2 * ns + ns] = jnp.concatenate(sr, axis=0).astype(_BF16)
            s_ref[r, d * 2 * ns + ns:(d + 1) * 2 * ns] = jnp.concatenate(si, axis=0).astype(_BF16)
            return 0

        lax.fori_loop(0, nk // KPB, add_carry, 0, unroll=SCAN_UNROLL)

    for j in range(PAIRS_PER_BLK):
        sp = jnp.concatenate([s_ref[:, q * ns + j * LANES:q * ns + (j + 1) * LANES] for q in range(4)],
                             axis=1)
        yp_ref[j] = yp_ref[j] + _dot(sp, wo_ref[0, j])

    for tg in range(tgroups):
        yc = jnp.concatenate([yp_ref[j, :, tg * LANES:(tg + 1) * LANES] for j in range(PAIRS_PER_BLK)],
                             axis=1)
        hi = yc.astype(_BF16)
        lo = (yc - hi.astype(_F32)).astype(_BF16)
        yf_ref[:, tg * tg_cols:(tg + 1) * tg_cols] = _dot(hi, perm_ref[...]) + _dot(lo, perm_ref[...])

    skip = skip_ref[0, 0]

    def scatter(k, _):
        r = pl.ds(pl.multiple_of(k * SUBLANES, SUBLANES), SUBLANES)
        for t in range(CHUNK_L):
            rows = seg_rows(k, t)
            y_ref[rows, :] = yf_ref[r, t * LANES:(t + 1) * LANES] + skip * u_ref[rows, :]
        return 0

    lax.fori_loop(0, nk, scatter, 0, unroll=2)
    seg = nk * CHUNK_L
    for s in range(NSEG):
        y_ref[s * pitch + seg:(s + 1) * pitch, :] = jnp.zeros((SEG_PAD, LANES), _F32)


def _pair_permutation():
    n = PAIRS_PER_BLK
    src = jnp.arange(n * LANES)
    tl, j, c = src // LANES, (src // PAIR_LANES) % n, src % PAIR_LANES
    dst = j * LANES + tl * PAIR_LANES + c
    return (dst[:, None] == jnp.arange(n * LANES)[None, :]).astype(_BF16)


def _s5_core(u_pad, t_mat, w_si, w_so, pw, skip, layer, nseq, seg):
    nk = seg // CHUNK_L
    pitch = seg + SEG_PAD
    m = nk * NSEG
    perm = _pair_permutation()
    seq_blk = pl.BlockSpec((NSEG * pitch, LANES), lambda b, s: (s, b))
    sq = pl.BlockSpec((1, PAIRS_PER_BLK, PAIR_K, PAIR_K), lambda b, s: (layer, b, 0, 0))
    return pl.pallas_call(
        functools.partial(_s5_kernel, nk=nk, pitch=pitch),
        grid=(NBLK, nseq),
        in_specs=[seq_blk, sq, sq, sq,
                  pl.BlockSpec((1, 2, 2, nk + 1, BLK_STATE), lambda b, s: (layer, 0, 0, 0, b)),
                  pl.BlockSpec((1, 1, 1, LANES), lambda b, s: (layer, b, 0, 0)),
                  pl.BlockSpec(perm.shape, lambda b, s: (0, 0))],
        out_specs=seq_blk,
        out_shape=jax.ShapeDtypeStruct(u_pad.shape, _F32),
        scratch_shapes=[pltpu.VMEM((m, CHUNK_L * LANES), _BF16),
                        pltpu.VMEM((PAIRS_PER_BLK, m, PAIR_K), _BF16),
                        pltpu.VMEM((2, m, 2 * BLK_STATE), _F32),
                        pltpu.VMEM((m, 4 * BLK_STATE), _BF16),
                        pltpu.VMEM((PAIRS_PER_BLK, m, PAIR_K), _F32),
                        pltpu.VMEM((m, CHUNK_L * LANES), _F32)],
        compiler_params=_params(2),
        name="s5_core",
    )(u_pad, t_mat, w_si, w_so, pw, skip, perm)


def _merge_kernel(y_ref, ga_ref, gb_ref, x_ref, wglu_ref, wo_ref, npost_ref, o_ref, *, seg):
    d = D_MODEL

    def sub(i, _):
        rows = pl.ds(pl.multiple_of(i * SUB_ROWS, SUB_ROWS), SUB_ROWS)
        z = _gelu(y_ref[rows, :]).astype(_BF16)
        b = _dot(z, wglu_ref[0, :, 0:d]) * _sigmoid(_dot(z, wglu_ref[0, :, d:2 * d]))
        m = (ga_ref[rows, :].astype(_F32) + gb_ref[rows, :].astype(_F32) * b).astype(_BF16)
        o_ref[rows, :] = x_ref[rows, :] + _rms(_dot(m, wo_ref[0]), npost_ref[0])
        return 0

    lax.fori_loop(0, seg // SUB_ROWS, sub, 0, unroll=SUB_UNROLL)


def _merge(y_pad, ga, gb, x, w_glu, w_o, npost, layer, seg):
    ntok = x.shape[0]
    pitch = seg + SEG_PAD
    tok = pl.BlockSpec((seg, D_MODEL), lambda i: (i, 0))
    return pl.pallas_call(
        functools.partial(_merge_kernel, seg=seg),
        grid=(ntok // seg,),
        in_specs=[pl.BlockSpec((pitch, D_MODEL), lambda i: (i, 0)), tok, tok, tok]
                 + [_layer_spec(w.shape, layer) for w in (w_glu, w_o, npost)],
        out_specs=tok,
        out_shape=jax.ShapeDtypeStruct((ntok, D_MODEL), _F32),
        compiler_params=_params(1),
        name="glu_merge",
    )(y_pad, ga, gb, x, w_glu, w_o, npost)


def _ffn_kernel(x_ref, npre_ref, w1_ref, w2_ref, npost_ref, o_ref):
    x = x_ref[...]
    h = _rms(x, npre_ref[0]).astype(_BF16)
    f = jnp.zeros(x.shape, _F32)
    for j in range(D_FF // D_MODEL):
        cs = slice(j * D_MODEL, (j + 1) * D_MODEL)
        r = jnp.maximum(_dot(h, w1_ref[0, :, cs]), 0.0)
        f = f + _dot((r * r).astype(_BF16), w2_ref[0, cs, :])
    o_ref[...] = x + _rms(f, npost_ref[0])


def _ffn(x, npre, w1, w2, npost, layer, rows):
    ntok = x.shape[0]
    tok = pl.BlockSpec((rows, D_MODEL), lambda i: (i, 0))
    return pl.pallas_call(
        _ffn_kernel,
        grid=(ntok // rows,),
        in_specs=[tok] + [_layer_spec(w.shape, layer) for w in (npre, w1, w2, npost)],
        out_specs=tok,
        out_shape=jax.ShapeDtypeStruct((ntok, D_MODEL), _F32),
        compiler_params=_params(1),
        name="ffn",
    )(x, npre, w1, w2, npost)


def _trunk(x, p, s5m):
    nseq, seqlen, d = x.shape
    seg = seqlen // NSEG
    assert d == D_MODEL and seg % SUB_ROWS == 0 and seqlen % FF_ROWS == 0
    x = x.reshape(nseq * seqlen, d)
    t_mat, w_si, w_so, pw = s5m
    for l in range(p["w_in"].shape[0]):
        ga, u_pad, gb = _inproj(x, p["norm_pre_mix"], p["w_in"], p["norm_v"], p["w_s"], p["bias"],
                                p["w_out_a"], l, seg)
        y_pad = _s5_core(u_pad, t_mat, w_si, w_so, pw, p["d_skip"], l, nseq, seg)
        x = _merge(y_pad, ga, gb, x, p["w_glu"], p["w_o"], p["norm_post_mix"], l, seg)
        x = _ffn(x, p["norm_pre_ff"], p["w_ff1"], p["w_ff2"], p["norm_post_ff"], l, FF_ROWS)
    return x.reshape(nseq, seqlen, d)


def kernel(x_prompt, x_sample, norm_pre_mix, w_in, norm_v, w_s, b_s, w_out_a, lam_re, lam_im,
           log_dt, b_re, b_im, c_re, c_im, d_skip, w_glu, w_o, norm_post_mix, norm_pre_ff,
           w_ff1, w_ff2, norm_post_ff):
    depth = w_in.shape[0]
    bf = lambda w: w.astype(_BF16)
    row = lambda v: v.reshape(depth, 1, D_MODEL)
    p = {
        "norm_pre_mix": row(norm_pre_mix), "w_in": bf(w_in), "norm_v": row(norm_v), "w_s": bf(w_s),
        "bias": jnp.repeat(jnp.swapaxes(b_s, 1, 2), LANES, axis=2),
        "w_out_a": bf(w_out_a), "d_skip": d_skip.reshape(depth, NBLK, 1, LANES),
        "w_glu": bf(w_glu), "w_o": bf(w_o), "norm_post_mix": row(norm_post_mix),
        "norm_pre_ff": row(norm_pre_ff), "w_ff1": bf(w_ff1), "w_ff2": bf(w_ff2),
        "norm_post_ff": row(norm_post_ff),
    }
    outs = []
    mats = {}
    for x in (x_prompt, x_sample):
        nk = x.shape[1] // (NSEG * CHUNK_L)
        if nk not in mats:
            mats[nk] = _s5_matrices(lam_re, lam_im, log_dt, b_re, b_im, c_re, c_im, nk)
        outs.append(_trunk(x, p, mats[nk]))
    return tuple(outs)
```

```python
import functools

import jax
import jax.numpy as jnp
from jax import lax
from jax.experimental import pallas as pl
from jax.experimental.pallas import tpu as pltpu

D_MODEL = 1024
GMLP_CHUNK = 128
A_GROUPS = 8
GROUPS = 64
GROUP_DIM = 16
STATE = 64
D_FF = 4 * D_MODEL
EPS = 1e-6

LANES = 128
SUBLANES = 8
CHUNK_L = 16
NSEG = SUBLANES
NBLK = D_MODEL // LANES
PAIR_LANES = 2 * GROUP_DIM
PAIRS_PER_BLK = LANES // PAIR_LANES
NPAIR = GROUPS // 2
PAIR_K = CHUNK_L * PAIR_LANES
PAIR_STATE = 2 * STATE
BLK_STATE = PAIRS_PER_BLK * PAIR_STATE
SEG_PAD = SUBLANES
ROW_BLK = 2 * SUBLANES
KPB = ROW_BLK // SUBLANES
SCAN_UNROLL = True
TOK_ROWS = 512
VMEM_LIMIT = 56 * 1024 * 1024

_F32 = jnp.float32
_BF16 = jnp.bfloat16


def _dot(a, b):
    return jnp.dot(a, b, preferred_element_type=_F32)


def _rms(x, g):
    return x * lax.rsqrt(jnp.mean(x * x, axis=-1, keepdims=True) + EPS) * g


def _sigmoid(x):
    return 1.0 / (1.0 + jnp.exp(-x))


def _gelu(x):
    return jax.nn.gelu(x, approximate=True)


def _layer_spec(shape, layer):
    nd = len(shape)
    return pl.BlockSpec((1,) + tuple(shape[1:]), lambda *_: (layer,) + (0,) * (nd - 1),
                        pipeline_mode=pl.Buffered(1))


def _params(n_axes):
    return pltpu.CompilerParams(
        dimension_semantics=("arbitrary",) * n_axes, vmem_limit_bytes=VMEM_LIMIT)


def _cmul(ar, ai, br, bi):
    return ar * br - ai * bi, ar * bi + ai * br


def _wprep_kernel(lre_ref, lim_ref, ldt_ref, bre_ref, bim_ref, cre_ref, cim_ref,
                  t_ref, wsi_ref, wso_ref, pw_ref,
                  xr_ref, xi_ref, yr_ref, yni_ref, eo_ref, *, nk):
    L = CHUNK_L
    lane = lax.broadcasted_iota(jnp.int32, (GROUP_DIM, LANES), 1)
    lane1 = lax.broadcasted_iota(jnp.int32, (1, LANES), 1)
    for d in range(2):
        a_pair = []
        for g2 in range(2):
            lre, lim = lre_ref[0, d, g2], lim_ref[0, d, g2]
            dt = jnp.exp(ldt_ref[0, d, g2])
            mag = jnp.exp(lre * dt)
            ar, ai = mag * jnp.cos(lim * dt), mag * jnp.sin(lim * dt)
            a_pair.append((ar, ai))
            den = lre * lre + lim * lim
            qr = ((ar - 1.0) * lre + ai * lim) / den
            qi = (ai * lre - (ar - 1.0) * lim) / den
            n2 = ar * ar + ai * ai
            ir, ii = ar / n2, -ai / n2
            keep = (lane >= g2 * STATE) & (lane < (g2 + 1) * STATE)
            pr, pi = _cmul(qr, qi, jnp.where(keep, bre_ref[0, d, g2], 0.0),
                           jnp.where(keep, bim_ref[0, d, g2], 0.0))
            nr, ni = pr, pi
            fr = jnp.where(keep, cre_ref[0, d, g2], 0.0)
            fi = jnp.where(keep, cim_ref[0, d, g2], 0.0)
            gr, gi = fr, fi

            def rows(k):
                return slice(k * PAIR_LANES + g2 * GROUP_DIM, k * PAIR_LANES + (g2 + 1) * GROUP_DIM)

            for k in range(L + 1):
                if k < L:
                    if d == 0:
                        xr_ref[0, rows(k), :] = nr
                        xi_ref[0, rows(k), :] = ni
                        yr_ref[0, rows(k), :] = fr
                        yni_ref[0, rows(k), :] = -fi
                        wsi_ref[0, 0, rows(L - 1 - k), 0:LANES] = pr.astype(_BF16)
                        wsi_ref[0, 0, rows(L - 1 - k), LANES:2 * LANES] = pi.astype(_BF16)
                    else:
                        xr_ref[1, rows(k), :] = pr
                        xi_ref[1, rows(k), :] = pi
                        yr_ref[1, rows(k), :] = gr
                        yni_ref[1, rows(k), :] = -gi
                        wsi_ref[0, 0, rows(k), 2 * LANES:3 * LANES] = pr.astype(_BF16)
                        wsi_ref[0, 0, rows(k), 3 * LANES:4 * LANES] = pi.astype(_BF16)
                if k >= 1:
                    t = k - 1 if d == 0 else L - k
                    eo_ref[2 * d, rows(t), :] = fr
                    eo_ref[2 * d + 1, rows(t), :] = -fi
                pr, pi = _cmul(ar, ai, pr, pi)
                nr, ni = _cmul(ir, ii, nr, ni)
                fr, fi = _cmul(ar, ai, fr, fi)
                gr, gi = _cmul(ir, ii, gr, gi)

        a_r = jnp.where(lane1 < STATE, a_pair[0][0], a_pair[1][0])
        a_i = jnp.where(lane1 < STATE, a_pair[0][1], a_pair[1][1])
        alr, ali = a_r, a_i
        for _ in range(L - 1):
            alr, ali = _cmul(alr, ali, a_r, a_i)
        one, zero = jnp.ones((1, LANES), _F32), jnp.zeros((1, LANES), _F32)
        pw_ref[0, d, 0, 0:1, :] = one
        pw_ref[0, d, 1, 0:1, :] = zero

        def power(k, carry):
            wr, wi = _cmul(carry[0], carry[1], alr, ali)
            pw_ref[0, d, 0, pl.ds(k + 1, 1), :] = wr
            pw_ref[0, d, 1, pl.ds(k + 1, 1), :] = wi
            return wr, wi

        lax.fori_loop(0, nk, power, (one, zero))

    def split(v):
        hi = v.astype(_BF16)
        return hi, (v - hi.astype(_F32)).astype(_BF16)

    def lag_matrix(d):
        xh, xl = split(jnp.concatenate([xr_ref[d], xi_ref[d]], axis=1))
        yh, yl = split(jnp.concatenate([yr_ref[d], yni_ref[d]], axis=1))
        return lax.dot_general(jnp.concatenate([xh, xh, xl], axis=1), jnp.concatenate([yh, yl, yh], axis=1),
                               (((1,), (1,)), ((), ())), preferred_element_type=_F32)

    shift = PAIR_LANES.bit_length() - 1
    tau = lax.broadcasted_iota(jnp.int32, (PAIR_K, PAIR_K), 0) >> shift
    t = lax.broadcasted_iota(jnp.int32, (PAIR_K, PAIR_K), 1) >> shift
    t_ref[0, 0] = (jnp.where(t >= tau, lag_matrix(0), 0.0)
                   + jnp.where(tau >= t, lag_matrix(1), 0.0)).astype(_BF16)
    for j in range(4):
        wso_ref[0, 0, j * LANES:(j + 1) * LANES, :] = eo_ref[j].T.astype(_BF16)


def _s5_matrices(lam_re, lam_im, log_dt, b_re, b_im, c_re, c_im, nk):
    depth = lam_re.shape[0]
    dup = lambda v: jnp.concatenate([v, v], axis=-1)
    row = lambda v: dup(v)[:, :, :, None, :]
    ldt = jnp.broadcast_to(log_dt[..., None, None], (depth, 2, GROUPS, 1, LANES))
    bt = lambda v: dup(jnp.swapaxes(v, -1, -2))
    vec = pl.BlockSpec((1, 2, 2, 1, LANES), lambda l, q: (l, 0, q, 0, 0))
    mat = pl.BlockSpec((1, 2, 2, GROUP_DIM, LANES), lambda l, q: (l, 0, q, 0, 0))
    sq = pl.BlockSpec((1, 1, PAIR_K, PAIR_K), lambda l, q: (l, q, 0, 0))
    sq_shape = jax.ShapeDtypeStruct((depth, NPAIR, PAIR_K, PAIR_K), _BF16)
    return pl.pallas_call(
        functools.partial(_wprep_kernel, nk=nk),
        grid=(depth, NPAIR),
        in_specs=[vec, vec, vec, mat, mat, mat, mat],
        out_specs=[sq, sq, sq,
                   pl.BlockSpec((1, 2, 2, nk + 1, LANES), lambda l, q: (l, 0, 0, 0, q))],
        out_shape=[sq_shape, sq_shape, sq_shape,
                   jax.ShapeDtypeStruct((depth, 2, 2, nk + 1, NPAIR * LANES), _F32)],
        scratch_shapes=[pltpu.VMEM((2, PAIR_K, LANES), _F32)] * 4
                       + [pltpu.VMEM((4, PAIR_K, LANES), _F32)],
        compiler_params=_params(2),
        name="s5_weight_prep",
    )(row(lam_re), row(lam_im), ldt, bt(b_re), bt(b_im), dup(c_re), dup(c_im))


def _inproj_kernel(x_ref, npm_ref, win_ref, nv_ref, ws_ref, bs_ref, woa_ref,
                   ga_ref, u_ref, gb_ref, mix_ref, *, rows, tiles_per_seg, seg):
    d = D_MODEL
    h = _rms(x_ref[...], npm_ref[0]).astype(_BF16)
    ua = _gelu(_dot(h, win_ref[0, :, 0:d]))
    va = _gelu(_dot(h, win_ref[0, :, d:2 * d]))
    vn = _rms(va, nv_ref[0]).astype(_BF16)
    for n in range(rows // GMLP_CHUNK):
        rs = slice(n * GMLP_CHUNK, (n + 1) * GMLP_CHUNK)
        for g in range(A_GROUPS):
            cs = slice(g * LANES, (g + 1) * LANES)
            mix_ref[rs, cs] = _dot(ws_ref[0, g], vn[rs, cs]) + bs_ref[0, :, cs]
    z = (ua * mix_ref[...]).astype(_BF16)
    a = _dot(z, woa_ref[0])
    r0 = pl.multiple_of((pl.program_id(0) % tiles_per_seg) * rows, rows)
    u_ref[pl.ds(r0, rows), :] = _dot(h, win_ref[0, :, 2 * d:3 * d])
    u_ref[seg:seg + SEG_PAD, :] = jnp.zeros((SEG_PAD, d), _F32)
    ga_ref[...] = (_sigmoid(_dot(h, win_ref[0, :, 3 * d:4 * d])) * a).astype(_BF16)
    gb_ref[...] = _sigmoid(_dot(h, win_ref[0, :, 4 * d:5 * d])).astype(_BF16)


def _inproj(x, npm, w_in, nv, w_s, bias, w_out_a, layer, seg):
    ntok = x.shape[0]
    rows = min(TOK_ROWS, seg)
    tps = seg // rows
    pitch = seg + SEG_PAD
    tok = pl.BlockSpec((rows, D_MODEL), lambda i: (i, 0))
    return pl.pallas_call(
        functools.partial(_inproj_kernel, rows=rows, tiles_per_seg=tps, seg=seg),
        grid=(ntok // rows,),
        in_specs=[tok] + [_layer_spec(w.shape, layer) for w in (npm, w_in, nv, w_s, bias, w_out_a)],
        out_specs=[tok, pl.BlockSpec((pitch, D_MODEL), lambda i: (i // tps, 0)), tok],
        out_shape=[jax.ShapeDtypeStruct((ntok, D_MODEL), _BF16),
                   jax.ShapeDtypeStruct((ntok // seg * pitch, D_MODEL), _F32),
                   jax.ShapeDtypeStruct((ntok, D_MODEL), _BF16)],
        scratch_shapes=[pltpu.VMEM((rows, D_MODEL), _F32)],
        compiler_params=_params(1),
        name="inproj_gmlp",
    )(x, npm, w_in, nv, w_s, bias, w_out_a)


def _s5_kernel(u_ref, t_ref, wi_ref, wo_ref, pw_ref, skip_ref, perm_ref, y_ref,
               raw_ref, lhs_ref, z_ref, s_ref, yp_ref, yf_ref, *, nk, pitch):
    ns = BLK_STATE
    tgroups = CHUNK_L // PAIRS_PER_BLK
    tg_cols = PAIRS_PER_BLK * LANES

    def seg_rows(k, tau):
        return pl.ds(k * CHUNK_L + tau, NSEG, stride=pitch)

    def gather(i, _):
        r = pl.ds(pl.multiple_of(i * ROW_BLK, ROW_BLK), ROW_BLK)
        for tau in range(CHUNK_L):
            v = jnp.concatenate([u_ref[seg_rows(i * KPB + kk, tau), :] for kk in range(KPB)], axis=0)
            raw_ref[r, tau * LANES:(tau + 1) * LANES] = v.astype(_BF16)
        return 0

    lax.fori_loop(0, nk // KPB, gather, 0, unroll=2)

    for tg in range(tgroups):
        pb = _dot(raw_ref[:, tg * tg_cols:(tg + 1) * tg_cols], perm_ref[...])
        for j in range(PAIRS_PER_BLK):
            lhs_ref[j, :, tg * LANES:(tg + 1) * LANES] = pb[:, j * LANES:(j + 1) * LANES].astype(_BF16)

    seg_id = lax.broadcasted_iota(jnp.int32, (NSEG, ns), 0)
    for d in range(2):
        for j in range(PAIRS_PER_BLK):
            zz = _dot(lhs_ref[j], wi_ref[0, j, :, d * 2 * LANES:(d + 1) * 2 * LANES])
            z_ref[d, :, j * LANES:(j + 1) * LANES] = zz[:, 0:LANES]
            z_ref[d, :, ns + j * LANES:ns + (j + 1) * LANES] = zz[:, LANES:2 * LANES]
    for j in range(PAIRS_PER_BLK):
        yp_ref[j] = _dot(lhs_ref[j], t_ref[0, j])

    for d in range(2):
        backward = d == 1
        al_r = jnp.broadcast_to(pw_ref[0, d, 0, 1:2, :], (NSEG, ns))
        al_i = jnp.broadcast_to(pw_ref[0, d, 1, 1:2, :], (NSEG, ns))

        def local_scan(j, carry):
            k = nk - 1 - j if backward else j
            r = pl.ds(pl.multiple_of(k * SUBLANES, SUBLANES), SUBLANES)
            sr, si = carry
            zr, zi = z_ref[d, r, 0:ns], z_ref[d, r, ns:2 * ns]
            z_ref[d, r, 0:ns] = sr
            z_ref[d, r, ns:2 * ns] = si
            pr, pi = _cmul(al_r, al_i, sr, si)
            return pr + zr, pi + zi

        zero = jnp.zeros((NSEG, ns), _F32)
        tot_r, tot_i = lax.fori_loop(0, nk, local_scan, (zero, zero), unroll=SCAN_UNROLL)

        as_r = jnp.broadcast_to(pw_ref[0, d, 0, nk:nk + 1, :], (NSEG, ns))
        as_i = jnp.broadcast_to(pw_ref[0, d, 1, nk:nk + 1, :], (NSEG, ns))
        edge = NSEG - 1 if backward else 0
        shift = NSEG - 1 if backward else 1
        cr, ci = zero, zero
        for _ in range(NSEG - 1):
            pr, pi = _cmul(as_r, as_i, cr, ci)
            cr = jnp.where(seg_id == edge, 0.0, pltpu.roll(pr + tot_r, shift, 0))
            ci = jnp.where(seg_id == edge, 0.0, pltpu.roll(pi + tot_i, shift, 0))

        def add_carry(i, _):
            r = pl.ds(pl.multiple_of(i * ROW_BLK, ROW_BLK), ROW_BLK)
            sr, si = [], []
            for kk in range(KPB):
                k = i * KPB + kk
                e = nk - 1 - k if backward else k
                pr, pi = _cmul(pw_ref[0, d, 0, pl.ds(e, 1), :], pw_ref[0, d, 1, pl.ds(e, 1), :], cr, ci)
                rk = pl.ds(pl.multiple_of(k * SUBLANES, SUBLANES), SUBLANES)
                sr.append(z_ref[d, rk, 0:ns] + pr)
                si.append(z_ref[d, rk, ns:2 * ns] + pi)
            s_ref[r, d * 2 * ns:d * 2 * ns + ns] = jnp.concatenate(sr, axis=0).astype(_BF16)
            s_ref[r, d * 2 * ns + ns:(d + 1) * 2 * ns] = jnp.concatenate(si, axis=0).astype(_BF16)
            return 0

        lax.fori_loop(0, nk // KPB, add_carry, 0, unroll=SCAN_UNROLL)

    for j in range(PAIRS_PER_BLK):
        sp = jnp.concatenate([s_ref[:, q * ns + j * LANES:q * ns + (j + 1) * LANES] for q in range(4)],
                             axis=1)
        yp_ref[j] = yp_ref[j] + _dot(sp, wo_ref[0, j])

    for tg in range(tgroups):
        yc = jnp.concatenate([yp_ref[j, :, tg * LANES:(tg + 1) * LANES] for j in range(PAIRS_PER_BLK)],
                             axis=1)
        hi = yc.astype(_BF16)
        lo = (yc - hi.astype(_F32)).astype(_BF16)
        yf_ref[:, tg * tg_cols:(tg + 1) * tg_cols] = _dot(hi, perm_ref[...]) + _dot(lo, perm_ref[...])

    skip = skip_ref[0, 0]

    def scatter(k, _):
        r = pl.ds(pl.multiple_of(k * SUBLANES, SUBLANES), SUBLANES)
        for t in range(CHUNK_L):
            rows = seg_rows(k, t)
            y_ref[rows, :] = yf_ref[r, t * LANES:(t + 1) * LANES] + skip * u_ref[rows, :]
        return 0

    lax.fori_loop(0, nk, scatter, 0, unroll=2)
    seg = nk * CHUNK_L
    for s in range(NSEG):
        y_ref[s * pitch + seg:(s + 1) * pitch, :] = jnp.zeros((SEG_PAD, LANES), _F32)


def _pair_permutation():
    n = PAIRS_PER_BLK
    src = jnp.arange(n * LANES)
    tl, j, c = src // LANES, (src // PAIR_LANES) % n, src % PAIR_LANES
    dst = j * LANES + tl * PAIR_LANES + c
    return (dst[:, None] == jnp.arange(n * LANES)[None, :]).astype(_BF16)


def _s5_core(u_pad, t_mat, w_si, w_so, pw, skip, layer, nseq, seg):
    nk = seg // CHUNK_L
    pitch = seg + SEG_PAD
    m = nk * NSEG
    perm = _pair_permutation()
    seq_blk = pl.BlockSpec((NSEG * pitch, LANES), lambda b, s: (s, b))
    sq = pl.BlockSpec((1, PAIRS_PER_BLK, PAIR_K, PAIR_K), lambda b, s: (layer, b, 0, 0))
    return pl.pallas_call(
        functools.partial(_s5_kernel, nk=nk, pitch=pitch),
        grid=(NBLK, nseq),
        in_specs=[seq_blk, sq, sq, sq,
                  pl.BlockSpec((1, 2, 2, nk + 1, BLK_STATE), lambda b, s: (layer, 0, 0, 0, b)),
                  pl.BlockSpec((1, 1, 1, LANES), lambda b, s: (layer, b, 0, 0)),
                  pl.BlockSpec(perm.shape, lambda b, s: (0, 0))],
        out_specs=seq_blk,
        out_shape=jax.ShapeDtypeStruct(u_pad.shape, _F32),
        scratch_shapes=[pltpu.VMEM((m, CHUNK_L * LANES), _BF16),
                        pltpu.VMEM((PAIRS_PER_BLK, m, PAIR_K), _BF16),
                        pltpu.VMEM((2, m, 2 * BLK_STATE), _F32),
                        pltpu.VMEM((m, 4 * BLK_STATE), _BF16),
                        pltpu.VMEM((PAIRS_PER_BLK, m, PAIR_K), _F32),
                        pltpu.VMEM((m, CHUNK_L * LANES), _F32)],
        compiler_params=_params(2),
        name="s5_core",
    )(u_pad, t_mat, w_si, w_so, pw, skip, perm)


def _merge_kernel(y_ref, ga_ref, gb_ref, x_ref, wglu_ref, wo_ref, npost_ref, o_ref, *, rows, tiles_per_seg):
    d = D_MODEL
    r0 = pl.multiple_of((pl.program_id(0) % tiles_per_seg) * rows, rows)
    z = _gelu(y_ref[pl.ds(r0, rows), :]).astype(_BF16)
    b = _dot(z, wglu_ref[0, :, 0:d]) * _sigmoid(_dot(z, wglu_ref[0, :, d:2 * d]))
    m = (ga_ref[...].astype(_F32) + gb_ref[...].astype(_F32) * b).astype(_BF16)
    o_ref[...] = x_ref[...] + _rms(_dot(m, wo_ref[0]), npost_ref[0])


def _merge(y_pad, ga, gb, x, w_glu, w_o, npost, layer, seg):
    ntok = x.shape[0]
    rows = min(TOK_ROWS, seg)
    tps = seg // rows
    pitch = seg + SEG_PAD
    tok = pl.BlockSpec((rows, D_MODEL), lambda i: (i, 0))
    return pl.pallas_call(
        functools.partial(_merge_kernel, rows=rows, tiles_per_seg=tps),
        grid=(ntok // rows,),
        in_specs=[pl.BlockSpec((pitch, D_MODEL), lambda i: (i // tps, 0)), tok, tok, tok]
                 + [_layer_spec(w.shape, layer) for w in (w_glu, w_o, npost)],
        out_specs=tok,
        out_shape=jax.ShapeDtypeStruct((ntok, D_MODEL), _F32),
        compiler_params=_params(1),
        name="glu_merge",
    )(y_pad, ga, gb, x, w_glu, w_o, npost)


def _ffn_kernel(x_ref, npre_ref, w1_ref, w2_ref, npost_ref, o_ref):
    x = x_ref[...]
    h = _rms(x, npre_ref[0]).astype(_BF16)
    f = jnp.zeros(x.shape, _F32)
    for j in range(D_FF // D_MODEL):
        cs = slice(j * D_MODEL, (j + 1) * D_MODEL)
        r = jnp.maximum(_dot(h, w1_ref[0, :, cs]), 0.0)
        f = f + _dot((r * r).astype(_BF16), w2_ref[0, cs, :])
    o_ref[...] = x + _rms(f, npost_ref[0])


def _ffn(x, npre, w1, w2, npost, layer, rows):
    ntok = x.shape[0]
    tok = pl.BlockSpec((rows, D_MODEL), lambda i: (i, 0))
    return pl.pallas_call(
        _ffn_kernel,
        grid=(ntok // rows,),
        in_specs=[tok] + [_layer_spec(w.shape, layer) for w in (npre, w1, w2, npost)],
        out_specs=tok,
        out_shape=jax.ShapeDtypeStruct((ntok, D_MODEL), _F32),
        compiler_params=_params(1),
        name="ffn",
    )(x, npre, w1, w2, npost)


def _trunk(x, p, s5m):
    nseq, seqlen, d = x.shape
    seg = seqlen // NSEG
    assert d == D_MODEL and seg % GMLP_CHUNK == 0 and seqlen % TOK_ROWS == 0
    x = x.reshape(nseq * seqlen, d)
    t_mat, w_si, w_so, pw = s5m
    for l in range(p["w_in"].shape[0]):
        ga, u_pad, gb = _inproj(x, p["norm_pre_mix"], p["w_in"], p["norm_v"], p["w_s"], p["bias"],
                                p["w_out_a"], l, seg)
        y_pad = _s5_core(u_pad, t_mat, w_si, w_so, pw, p["d_skip"], l, nseq, seg)
        x = _merge(y_pad, ga, gb, x, p["w_glu"], p["w_o"], p["norm_post_mix"], l, seg)
        x = _ffn(x, p["norm_pre_ff"], p["w_ff1"], p["w_ff2"], p["norm_post_ff"], l, TOK_ROWS)
    return x.reshape(nseq, seqlen, d)


def kernel(x_prompt, x_sample, norm_pre_mix, w_in, norm_v, w_s, b_s, w_out_a, lam_re, lam_im,
           log_dt, b_re, b_im, c_re, c_im, d_skip, w_glu, w_o, norm_post_mix, norm_pre_ff,
           w_ff1, w_ff2, norm_post_ff):
    depth = w_in.shape[0]
    bf = lambda w: w.astype(_BF16)
    row = lambda v: v.reshape(depth, 1, D_MODEL)
    p = {
        "norm_pre_mix": row(norm_pre_mix), "w_in": bf(w_in), "norm_v": row(norm_v), "w_s": bf(w_s),
        "bias": jnp.repeat(jnp.swapaxes(b_s, 1, 2), LANES, axis=2),
        "w_out_a": bf(w_out_a), "d_skip": d_skip.reshape(depth, NBLK, 1, LANES),
        "w_glu": bf(w_glu), "w_o": bf(w_o), "norm_post_mix": row(norm_post_mix),
        "norm_pre_ff": row(norm_pre_ff), "w_ff1": bf(w_ff1), "w_ff2": bf(w_ff2),
        "norm_post_ff": row(norm_post_ff),
    }
    outs = []
    mats = {}
    for x in (x_prompt, x_sample):
        nk = x.shape[1] // (NSEG * CHUNK_L)
        if nk not in mats:
            mats[nk] = _s5_matrices(lam_re, lam_im, log_dt, b_re, b_im, c_re, c_im, nk)
        outs.append(_trunk(x, p, mats[nk]))
    return tuple(outs)
```

```python
import functools

import jax
import jax.numpy as jnp
from jax import lax
from jax.experimental import pallas as pl
from jax.experimental.pallas import tpu as pltpu

D_MODEL = 1024
GMLP_CHUNK = 128
A_GROUPS = 8
GROUPS = 64
GROUP_DIM = 16
STATE = 64
D_FF = 4 * D_MODEL
EPS = 1e-6

LANES = 128
SUBLANES = 8
CHUNK_L = 16
NSEG = SUBLANES
NBLK = D_MODEL // LANES
PAIR_LANES = 2 * GROUP_DIM
PAIRS_PER_BLK = LANES // PAIR_LANES
NPAIR = GROUPS // 2
PAIR_K = CHUNK_L * PAIR_LANES
PAIR_STATE = 2 * STATE
BLK_STATE = PAIRS_PER_BLK * PAIR_STATE
SEG_PAD = SUBLANES
ROW_BLK = 2 * SUBLANES
KPB = ROW_BLK // SUBLANES
SCAN_UNROLL = True
TOK_ROWS = 512
ROW_PARTS = 2
VMEM_LIMIT = 56 * 1024 * 1024

_F32 = jnp.float32
_BF16 = jnp.bfloat16


def _dot(a, b):
    return jnp.dot(a, b, preferred_element_type=_F32)


def _rms(x, g):
    return x * lax.rsqrt(jnp.mean(x * x, axis=-1, keepdims=True) + EPS) * g


def _sigmoid(x):
    return 1.0 / (1.0 + jnp.exp(-x))


def _gelu(x):
    return jax.nn.gelu(x, approximate=True)


def _layer_spec(shape, layer):
    nd = len(shape)
    return pl.BlockSpec((1,) + tuple(shape[1:]), lambda *_: (layer,) + (0,) * (nd - 1),
                        pipeline_mode=pl.Buffered(1))


def _params(n_axes):
    return pltpu.CompilerParams(
        dimension_semantics=("arbitrary",) * n_axes, vmem_limit_bytes=VMEM_LIMIT)


def _cmul(ar, ai, br, bi):
    return ar * br - ai * bi, ar * bi + ai * br


def _wprep_kernel(lre_ref, lim_ref, ldt_ref, bre_ref, bim_ref, cre_ref, cim_ref,
                  t_ref, wsi_ref, wso_ref, pw_ref,
                  xr_ref, xi_ref, yr_ref, yni_ref, eo_ref, *, nk):
    L = CHUNK_L
    lane = lax.broadcasted_iota(jnp.int32, (GROUP_DIM, LANES), 1)
    lane1 = lax.broadcasted_iota(jnp.int32, (1, LANES), 1)
    for d in range(2):
        a_pair = []
        for g2 in range(2):
            lre, lim = lre_ref[0, d, g2], lim_ref[0, d, g2]
            dt = jnp.exp(ldt_ref[0, d, g2])
            mag = jnp.exp(lre * dt)
            ar, ai = mag * jnp.cos(lim * dt), mag * jnp.sin(lim * dt)
            a_pair.append((ar, ai))
            den = lre * lre + lim * lim
            qr = ((ar - 1.0) * lre + ai * lim) / den
            qi = (ai * lre - (ar - 1.0) * lim) / den
            n2 = ar * ar + ai * ai
            ir, ii = ar / n2, -ai / n2
            keep = (lane >= g2 * STATE) & (lane < (g2 + 1) * STATE)
            pr, pi = _cmul(qr, qi, jnp.where(keep, bre_ref[0, d, g2], 0.0),
                           jnp.where(keep, bim_ref[0, d, g2], 0.0))
            nr, ni = pr, pi
            fr = jnp.where(keep, cre_ref[0, d, g2], 0.0)
            fi = jnp.where(keep, cim_ref[0, d, g2], 0.0)
            gr, gi = fr, fi

            def rows(k):
                return slice(k * PAIR_LANES + g2 * GROUP_DIM, k * PAIR_LANES + (g2 + 1) * GROUP_DIM)

            for k in range(L + 1):
                if k < L:
                    if d == 0:
                        xr_ref[0, rows(k), :] = nr
                        xi_ref[0, rows(k), :] = ni
                        yr_ref[0, rows(k), :] = fr
                        yni_ref[0, rows(k), :] = -fi
                        wsi_ref[0, 0, rows(L - 1 - k), 0:LANES] = pr.astype(_BF16)
                        wsi_ref[0, 0, rows(L - 1 - k), LANES:2 * LANES] = pi.astype(_BF16)
                    else:
                        xr_ref[1, rows(k), :] = pr
                        xi_ref[1, rows(k), :] = pi
                        yr_ref[1, rows(k), :] = gr
                        yni_ref[1, rows(k), :] = -gi
                        wsi_ref[0, 0, rows(k), 2 * LANES:3 * LANES] = pr.astype(_BF16)
                        wsi_ref[0, 0, rows(k), 3 * LANES:4 * LANES] = pi.astype(_BF16)
                if k >= 1:
                    t = k - 1 if d == 0 else L - k
                    eo_ref[2 * d, rows(t), :] = fr
                    eo_ref[2 * d + 1, rows(t), :] = -fi
                pr, pi = _cmul(ar, ai, pr, pi)
                nr, ni = _cmul(ir, ii, nr, ni)
                fr, fi = _cmul(ar, ai, fr, fi)
                gr, gi = _cmul(ir, ii, gr, gi)

        a_r = jnp.where(lane1 < STATE, a_pair[0][0], a_pair[1][0])
        a_i = jnp.where(lane1 < STATE, a_pair[0][1], a_pair[1][1])
        alr, ali = a_r, a_i
        for _ in range(L - 1):
            alr, ali = _cmul(alr, ali, a_r, a_i)
        one, zero = jnp.ones((1, LANES), _F32), jnp.zeros((1, LANES), _F32)
        pw_ref[0, d, 0, 0:1, :] = one
        pw_ref[0, d, 1, 0:1, :] = zero

        def power(k, carry):
            wr, wi = _cmul(carry[0], carry[1], alr, ali)
            pw_ref[0, d, 0, pl.ds(k + 1, 1), :] = wr
            pw_ref[0, d, 1, pl.ds(k + 1, 1), :] = wi
            return wr, wi

        lax.fori_loop(0, nk, power, (one, zero))

    def split(v):
        hi = v.astype(_BF16)
        return hi, (v - hi.astype(_F32)).astype(_BF16)

    def lag_matrix(d):
        xh, xl = split(jnp.concatenate([xr_ref[d], xi_ref[d]], axis=1))
        yh, yl = split(jnp.concatenate([yr_ref[d], yni_ref[d]], axis=1))
        return lax.dot_general(jnp.concatenate([xh, xh, xl], axis=1), jnp.concatenate([yh, yl, yh], axis=1),
                               (((1,), (1,)), ((), ())), preferred_element_type=_F32)

    shift = PAIR_LANES.bit_length() - 1
    tau = lax.broadcasted_iota(jnp.int32, (PAIR_K, PAIR_K), 0) >> shift
    t = lax.broadcasted_iota(jnp.int32, (PAIR_K, PAIR_K), 1) >> shift
    t_ref[0, 0] = (jnp.where(t >= tau, lag_matrix(0), 0.0)
                   + jnp.where(tau >= t, lag_matrix(1), 0.0)).astype(_BF16)
    for j in range(4):
        wso_ref[0, 0, j * LANES:(j + 1) * LANES, :] = eo_ref[j].T.astype(_BF16)


def _s5_matrices(lam_re, lam_im, log_dt, b_re, b_im, c_re, c_im, nk):
    depth = lam_re.shape[0]
    dup = lambda v: jnp.concatenate([v, v], axis=-1)
    row = lambda v: dup(v)[:, :, :, None, :]
    ldt = jnp.broadcast_to(log_dt[..., None, None], (depth, 2, GROUPS, 1, LANES))
    bt = lambda v: dup(jnp.swapaxes(v, -1, -2))
    vec = pl.BlockSpec((1, 2, 2, 1, LANES), lambda l, q: (l, 0, q, 0, 0))
    mat = pl.BlockSpec((1, 2, 2, GROUP_DIM, LANES), lambda l, q: (l, 0, q, 0, 0))
    sq = pl.BlockSpec((1, 1, PAIR_K, PAIR_K), lambda l, q: (l, q, 0, 0))
    sq_shape = jax.ShapeDtypeStruct((depth, NPAIR, PAIR_K, PAIR_K), _BF16)
    return pl.pallas_call(
        functools.partial(_wprep_kernel, nk=nk),
        grid=(depth, NPAIR),
        in_specs=[vec, vec, vec, mat, mat, mat, mat],
        out_specs=[sq, sq, sq,
                   pl.BlockSpec((1, 2, 2, nk + 1, LANES), lambda l, q: (l, 0, 0, 0, q))],
        out_shape=[sq_shape, sq_shape, sq_shape,
                   jax.ShapeDtypeStruct((depth, 2, 2, nk + 1, NPAIR * LANES), _F32)],
        scratch_shapes=[pltpu.VMEM((2, PAIR_K, LANES), _F32)] * 4
                       + [pltpu.VMEM((4, PAIR_K, LANES), _F32)],
        compiler_params=_params(2),
        name="s5_weight_prep",
    )(row(lam_re), row(lam_im), ldt, bt(b_re), bt(b_im), dup(c_re), dup(c_im))


def _inproj_kernel(x_ref, npm_ref, win_ref, nv_ref, ws_ref, bs_ref, woa_ref,
                   ga_ref, u_ref, gb_ref, mix_ref, *, rows, tiles_per_seg, seg):
    d = D_MODEL
    nchunk = rows // GMLP_CHUNK
    h = _rms(x_ref[...], npm_ref[0]).astype(_BF16)
    vn = _rms(_gelu(_dot(h, win_ref[0, :, d:2 * d])), nv_ref[0]).astype(_BF16)
    r0 = pl.multiple_of((pl.program_id(0) % tiles_per_seg) * rows, rows)
    u_ref[pl.ds(r0, rows), :] = _dot(h, win_ref[0, :, 2 * d:3 * d])
    u_ref[seg:seg + SEG_PAD, :] = jnp.zeros((SEG_PAD, d), _F32)
    gb_ref[...] = _sigmoid(_dot(h, win_ref[0, :, 4 * d:5 * d])).astype(_BF16)
    ua = _gelu(_dot(h, win_ref[0, :, 0:d]))
    for g in range(A_GROUPS):
        cs = slice(g * LANES, (g + 1) * LANES)
        v_all = jnp.concatenate([vn[n * GMLP_CHUNK:(n + 1) * GMLP_CHUNK, cs] for n in range(nchunk)], axis=1)
        mixed = _dot(ws_ref[0, g], v_all)
        for n in range(nchunk):
            mix_ref[n * GMLP_CHUNK:(n + 1) * GMLP_CHUNK, cs] = (
                mixed[:, n * LANES:(n + 1) * LANES] + bs_ref[0, :, cs])
    z = (ua * mix_ref[...]).astype(_BF16)
    a = _dot(z, woa_ref[0])
    ga_ref[...] = (_sigmoid(_dot(h, win_ref[0, :, 3 * d:4 * d])) * a).astype(_BF16)


def _inproj(x, npm, w_in, nv, w_s, bias, w_out_a, layer, seg):
    ntok = x.shape[0]
    rows = min(TOK_ROWS, seg)
    tps = seg // rows
    pitch = seg + SEG_PAD
    tok = pl.BlockSpec((rows, D_MODEL), lambda i: (i, 0))
    return pl.pallas_call(
        functools.partial(_inproj_kernel, rows=rows, tiles_per_seg=tps, seg=seg),
        grid=(ntok // rows,),
        in_specs=[tok] + [_layer_spec(w.shape, layer) for w in (npm, w_in, nv, w_s, bias, w_out_a)],
        out_specs=[tok, pl.BlockSpec((pitch, D_MODEL), lambda i: (i // tps, 0)), tok],
        out_shape=[jax.ShapeDtypeStruct((ntok, D_MODEL), _BF16),
                   jax.ShapeDtypeStruct((ntok // seg * pitch, D_MODEL), _F32),
                   jax.ShapeDtypeStruct((ntok, D_MODEL), _BF16)],
        scratch_shapes=[pltpu.VMEM((rows, D_MODEL), _F32)],
        compiler_params=_params(1),
        name="inproj_gmlp",
    )(x, npm, w_in, nv, w_s, bias, w_out_a)


def _s5_kernel(u_ref, t_ref, wi_ref, wo_ref, pw_ref, skip_ref, perm_ref, y_ref,
               raw_ref, lhs_ref, z_ref, s_ref, yp_ref, yf_ref, *, nk, pitch):
    ns = BLK_STATE
    tgroups = CHUNK_L // PAIRS_PER_BLK
    tg_cols = PAIRS_PER_BLK * LANES

    def seg_rows(k, tau):
        return pl.ds(k * CHUNK_L + tau, NSEG, stride=pitch)

    def gather(i, _):
        r = pl.ds(pl.multiple_of(i * ROW_BLK, ROW_BLK), ROW_BLK)
        for tau in range(CHUNK_L):
            v = jnp.concatenate([u_ref[seg_rows(i * KPB + kk, tau), :] for kk in range(KPB)], axis=0)
            raw_ref[r, tau * LANES:(tau + 1) * LANES] = v.astype(_BF16)
        return 0

    lax.fori_loop(0, nk // KPB, gather, 0, unroll=2)

    for tg in range(tgroups):
        pb = _dot(raw_ref[:, tg * tg_cols:(tg + 1) * tg_cols], perm_ref[...])
        for j in range(PAIRS_PER_BLK):
            lhs_ref[j, :, tg * LANES:(tg + 1) * LANES] = pb[:, j * LANES:(j + 1) * LANES].astype(_BF16)

    seg_id = lax.broadcasted_iota(jnp.int32, (NSEG, ns), 0)
    for d in range(2):
        for j in range(PAIRS_PER_BLK):
            zz = _dot(lhs_ref[j], wi_ref[0, j, :, d * 2 * LANES:(d + 1) * 2 * LANES])
            z_ref[d, :, j * LANES:(j + 1) * LANES] = zz[:, 0:LANES]
            z_ref[d, :, ns + j * LANES:ns + (j + 1) * LANES] = zz[:, LANES:2 * LANES]
    for j in range(PAIRS_PER_BLK):
        yp_ref[j] = _dot(lhs_ref[j], t_ref[0, j])

    for d in range(2):
        backward = d == 1
        al_r = jnp.broadcast_to(pw_ref[0, d, 0, 1:2, :], (NSEG, ns))
        al_i = jnp.broadcast_to(pw_ref[0, d, 1, 1:2, :], (NSEG, ns))

        def local_scan(j, carry):
            k = nk - 1 - j if backward else j
            r = pl.ds(pl.multiple_of(k * SUBLANES, SUBLANES), SUBLANES)
            sr, si = carry
            zr, zi = z_ref[d, r, 0:ns], z_ref[d, r, ns:2 * ns]
            z_ref[d, r, 0:ns] = sr
            z_ref[d, r, ns:2 * ns] = si
            pr, pi = _cmul(al_r, al_i, sr, si)
            return pr + zr, pi + zi

        zero = jnp.zeros((NSEG, ns), _F32)
        tot_r, tot_i = lax.fori_loop(0, nk, local_scan, (zero, zero), unroll=SCAN_UNROLL)

        as_r = jnp.broadcast_to(pw_ref[0, d, 0, nk:nk + 1, :], (NSEG, ns))
        as_i = jnp.broadcast_to(pw_ref[0, d, 1, nk:nk + 1, :], (NSEG, ns))
        edge = NSEG - 1 if backward else 0
        shift = NSEG - 1 if backward else 1
        cr, ci = zero, zero
        for _ in range(NSEG - 1):
            pr, pi = _cmul(as_r, as_i, cr, ci)
            cr = jnp.where(seg_id == edge, 0.0, pltpu.roll(pr + tot_r, shift, 0))
            ci = jnp.where(seg_id == edge, 0.0, pltpu.roll(pi + tot_i, shift, 0))

        def add_carry(i, _):
            r = pl.ds(pl.multiple_of(i * ROW_BLK, ROW_BLK), ROW_BLK)
            sr, si = [], []
            for kk in range(KPB):
                k = i * KPB + kk
                e = nk - 1 - k if backward else k
                pr, pi = _cmul(pw_ref[0, d, 0, pl.ds(e, 1), :], pw_ref[0, d, 1, pl.ds(e, 1), :], cr, ci)
                rk = pl.ds(pl.multiple_of(k * SUBLANES, SUBLANES), SUBLANES)
                sr.append(z_ref[d, rk, 0:ns] + pr)
                si.append(z_ref[d, rk, ns:2 * ns] + pi)
            s_ref[r, d * 2 * ns:d * 2 * ns + ns] = jnp.concatenate(sr, axis=0).astype(_BF16)
            s_ref[r, d * 2 * ns + ns:(d + 1) * 2 * ns] = jnp.concatenate(si, axis=0).astype(_BF16)
            return 0

        lax.fori_loop(0, nk // KPB, add_carry, 0, unroll=SCAN_UNROLL)

    for j in range(PAIRS_PER_BLK):
        sp = jnp.concatenate([s_ref[:, q * ns + j * LANES:q * ns + (j + 1) * LANES] for q in range(4)],
                             axis=1)
        yp_ref[j] = yp_ref[j] + _dot(sp, wo_ref[0, j])

    for tg in range(tgroups):
        yc = jnp.concatenate([yp_ref[j, :, tg * LANES:(tg + 1) * LANES] for j in range(PAIRS_PER_BLK)],
                             axis=1)
        yf_ref[:, tg * tg_cols:(tg + 1) * tg_cols] = _dot(yc.astype(_BF16), perm_ref[...])

    skip = skip_ref[0, 0]

    def scatter(k, _):
        r = pl.ds(pl.multiple_of(k * SUBLANES, SUBLANES), SUBLANES)
        for t in range(CHUNK_L):
            rows = seg_rows(k, t)
            y_ref[rows, :] = yf_ref[r, t * LANES:(t + 1) * LANES] + skip * u_ref[rows, :]
        return 0

    lax.fori_loop(0, nk, scatter, 0, unroll=2)
    seg = nk * CHUNK_L
    for s in range(NSEG):
        y_ref[s * pitch + seg:(s + 1) * pitch, :] = jnp.zeros((SEG_PAD, LANES), _F32)


def _pair_permutation():
    n = PAIRS_PER_BLK
    src = jnp.arange(n * LANES)
    tl, j, c = src // LANES, (src // PAIR_LANES) % n, src % PAIR_LANES
    dst = j * LANES + tl * PAIR_LANES + c
    return (dst[:, None] == jnp.arange(n * LANES)[None, :]).astype(_BF16)


def _s5_core(u_pad, t_mat, w_si, w_so, pw, skip, layer, nseq, seg):
    nk = seg // CHUNK_L
    pitch = seg + SEG_PAD
    m = nk * NSEG
    perm = _pair_permutation()
    seq_blk = pl.BlockSpec((NSEG * pitch, LANES), lambda b, s: (s, b))
    sq = pl.BlockSpec((1, PAIRS_PER_BLK, PAIR_K, PAIR_K), lambda b, s: (layer, b, 0, 0))
    return pl.pallas_call(
        functools.partial(_s5_kernel, nk=nk, pitch=pitch),
        grid=(NBLK, nseq),
        in_specs=[seq_blk, sq, sq, sq,
                  pl.BlockSpec((1, 2, 2, nk + 1, BLK_STATE), lambda b, s: (layer, 0, 0, 0, b)),
                  pl.BlockSpec((1, 1, 1, LANES), lambda b, s: (layer, b, 0, 0)),
                  pl.BlockSpec(perm.shape, lambda b, s: (0, 0))],
        out_specs=seq_blk,
        out_shape=jax.ShapeDtypeStruct(u_pad.shape, _F32),
        scratch_shapes=[pltpu.VMEM((m, CHUNK_L * LANES), _BF16),
                        pltpu.VMEM((PAIRS_PER_BLK, m, PAIR_K), _BF16),
                        pltpu.VMEM((2, m, 2 * BLK_STATE), _F32),
                        pltpu.VMEM((m, 4 * BLK_STATE), _BF16),
                        pltpu.VMEM((PAIRS_PER_BLK, m, PAIR_K), _F32),
                        pltpu.VMEM((m, CHUNK_L * LANES), _F32)],
        compiler_params=_params(2),
        name="s5_core",
    )(u_pad, t_mat, w_si, w_so, pw, skip, perm)


def _merge_kernel(y_ref, ga_ref, gb_ref, x_ref, wglu_ref, wo_ref, npost_ref, o_ref, *, rows, tiles_per_seg):
    d = D_MODEL
    r0 = pl.multiple_of((pl.program_id(0) % tiles_per_seg) * rows, rows)
    parts = [slice(q * (rows // ROW_PARTS), (q + 1) * (rows // ROW_PARTS)) for q in range(ROW_PARTS)]
    z = [_gelu(y_ref[pl.ds(r0 + p.start, p.stop - p.start), :]).astype(_BF16) for p in parts]
    g1 = [_dot(zq, wglu_ref[0, :, 0:d]) for zq in z]
    g2 = [_dot(zq, wglu_ref[0, :, d:2 * d]) for zq in z]
    m = [(ga_ref[p, :].astype(_F32) + gb_ref[p, :].astype(_F32) * (a * _sigmoid(b))).astype(_BF16)
         for p, a, b in zip(parts, g1, g2)]
    for p, mq in zip(parts, m):
        o_ref[p, :] = x_ref[p, :] + _rms(_dot(mq, wo_ref[0]), npost_ref[0])


def _merge(y_pad, ga, gb, x, w_glu, w_o, npost, layer, seg):
    ntok = x.shape[0]
    rows = min(TOK_ROWS, seg)
    tps = seg // rows
    pitch = seg + SEG_PAD
    tok = pl.BlockSpec((rows, D_MODEL), lambda i: (i, 0))
    return pl.pallas_call(
        functools.partial(_merge_kernel, rows=rows, tiles_per_seg=tps),
        grid=(ntok // rows,),
        in_specs=[pl.BlockSpec((pitch, D_MODEL), lambda i: (i // tps, 0)), tok, tok, tok]
                 + [_layer_spec(w.shape, layer) for w in (w_glu, w_o, npost)],
        out_specs=tok,
        out_shape=jax.ShapeDtypeStruct((ntok, D_MODEL), _F32),
        compiler_params=_params(1),
        name="glu_merge",
    )(y_pad, ga, gb, x, w_glu, w_o, npost)


def _ffn_kernel(x_ref, npre_ref, w1_ref, w2_ref, npost_ref, o_ref):
    x = x_ref[...]
    h = _rms(x, npre_ref[0]).astype(_BF16)
    f = jnp.zeros(x.shape, _F32)
    for j in range(D_FF // D_MODEL):
        cs = slice(j * D_MODEL, (j + 1) * D_MODEL)
        r = jnp.maximum(_dot(h, w1_ref[0, :, cs]), 0.0)
        f = f + _dot((r * r).astype(_BF16), w2_ref[0, cs, :])
    o_ref[...] = x + _rms(f, npost_ref[0])


def _ffn(x, npre, w1, w2, npost, layer, rows):
    ntok = x.shape[0]
    tok = pl.BlockSpec((rows, D_MODEL), lambda i: (i, 0))
    return pl.pallas_call(
        _ffn_kernel,
        grid=(ntok // rows,),
        in_specs=[tok] + [_layer_spec(w.shape, layer) for w in (npre, w1, w2, npost)],
        out_specs=tok,
        out_shape=jax.ShapeDtypeStruct((ntok, D_MODEL), _F32),
        compiler_params=_params(1),
        name="ffn",
    )(x, npre, w1, w2, npost)


def _trunk(x, p, s5m):
    nseq, seqlen, d = x.shape
    seg = seqlen // NSEG
    assert d == D_MODEL and seg % GMLP_CHUNK == 0 and seqlen % TOK_ROWS == 0
    x = x.reshape(nseq * seqlen, d)
    t_mat, w_si, w_so, pw = s5m
    for l in range(p["w_in"].shape[0]):
        ga, u_pad, gb = _inproj(x, p["norm_pre_mix"], p["w_in"], p["norm_v"], p["w_s"], p["bias"],
                                p["w_out_a"], l, seg)
        y_pad = _s5_core(u_pad, t_mat, w_si, w_so, pw, p["d_skip"], l, nseq, seg)
        x = _merge(y_pad, ga, gb, x, p["w_glu"], p["w_o"], p["norm_post_mix"], l, seg)
        x = _ffn(x, p["norm_pre_ff"], p["w_ff1"], p["w_ff2"], p["norm_post_ff"], l, TOK_ROWS)
    return x.reshape(nseq, seqlen, d)


def kernel(x_prompt, x_sample, norm_pre_mix, w_in, norm_v, w_s, b_s, w_out_a, lam_re, lam_im,
           log_dt, b_re, b_im, c_re, c_im, d_skip, w_glu, w_o, norm_post_mix, norm_pre_ff,
           w_ff1, w_ff2, norm_post_ff):
    depth = w_in.shape[0]
    bf = lambda w: w.astype(_BF16)
    row = lambda v: v.reshape(depth, 1, D_MODEL)
    p = {
        "norm_pre_mix": row(norm_pre_mix), "w_in": bf(w_in), "norm_v": row(norm_v), "w_s": bf(w_s),
        "bias": jnp.repeat(jnp.swapaxes(b_s, 1, 2), LANES, axis=2),
        "w_out_a": bf(w_out_a), "d_skip": d_skip.reshape(depth, NBLK, 1, LANES),
        "w_glu": bf(w_glu), "w_o": bf(w_o), "norm_post_mix": row(norm_post_mix),
        "norm_pre_ff": row(norm_pre_ff), "w_ff1": bf(w_ff1), "w_ff2": bf(w_ff2),
        "norm_post_ff": row(norm_post_ff),
    }
    outs = []
    mats = {}
    for x in (x_prompt, x_sample):
        nk = x.shape[1] // (NSEG * CHUNK_L)
        if nk not in mats:
            mats[nk] = _s5_matrices(lam_re, lam_im, log_dt, b_re, b_im, c_re, c_im, nk)
        outs.append(_trunk(x, p, mats[nk]))
    return tuple(outs)
```

```python
import functools

import jax
import jax.numpy as jnp
from jax import lax
from jax.experimental import pallas as pl
from jax.experimental.pallas import tpu as pltpu

D_MODEL = 1024
GMLP_CHUNK = 128
A_GROUPS = 8
GROUPS = 64
GROUP_DIM = 16
STATE = 64
D_FF = 4 * D_MODEL
EPS = 1e-6

LANES = 128
SUBLANES = 8
CHUNK_L = 16
NSEG = SUBLANES
NBLK = D_MODEL // LANES
PAIR_LANES = 2 * GROUP_DIM
PAIRS_PER_BLK = LANES // PAIR_LANES
NPAIR = GROUPS // 2
PAIR_K = CHUNK_L * PAIR_LANES
PAIR_STATE = 2 * STATE
BLK_STATE = PAIRS_PER_BLK * PAIR_STATE
SEG_PAD = SUBLANES
ROW_BLK = 2 * SUBLANES
KPB = ROW_BLK // SUBLANES
SCAN_UNROLL = True
TOK_ROWS = 512
ROW_PARTS = 2
FF_ROWS = 1024
VMEM_LIMIT = 56 * 1024 * 1024

_F32 = jnp.float32
_BF16 = jnp.bfloat16


def _dot(a, b):
    return jnp.dot(a, b, preferred_element_type=_F32)


def _rms(x, g):
    return x * lax.rsqrt(jnp.mean(x * x, axis=-1, keepdims=True) + EPS) * g


def _sigmoid(x):
    return 1.0 / (1.0 + jnp.exp(-x))


def _gelu(x):
    return jax.nn.gelu(x, approximate=True)


def _layer_spec(shape, layer):
    nd = len(shape)
    return pl.BlockSpec((1,) + tuple(shape[1:]), lambda *_: (layer,) + (0,) * (nd - 1),
                        pipeline_mode=pl.Buffered(1))


def _params(n_axes):
    return pltpu.CompilerParams(
        dimension_semantics=("arbitrary",) * n_axes, vmem_limit_bytes=VMEM_LIMIT)


def _cmul(ar, ai, br, bi):
    return ar * br - ai * bi, ar * bi + ai * br


def _wprep_kernel(lre_ref, lim_ref, ldt_ref, bre_ref, bim_ref, cre_ref, cim_ref,
                  t_ref, wsi_ref, wso_ref, pw_ref,
                  xr_ref, xi_ref, yr_ref, yni_ref, eo_ref, *, nk):
    L = CHUNK_L
    lane = lax.broadcasted_iota(jnp.int32, (GROUP_DIM, LANES), 1)
    lane1 = lax.broadcasted_iota(jnp.int32, (1, LANES), 1)
    for d in range(2):
        a_pair = []
        for g2 in range(2):
            lre, lim = lre_ref[0, d, g2], lim_ref[0, d, g2]
            dt = jnp.exp(ldt_ref[0, d, g2])
            mag = jnp.exp(lre * dt)
            ar, ai = mag * jnp.cos(lim * dt), mag * jnp.sin(lim * dt)
            a_pair.append((ar, ai))
            den = lre * lre + lim * lim
            qr = ((ar - 1.0) * lre + ai * lim) / den
            qi = (ai * lre - (ar - 1.0) * lim) / den
            n2 = ar * ar + ai * ai
            ir, ii = ar / n2, -ai / n2
            keep = (lane >= g2 * STATE) & (lane < (g2 + 1) * STATE)
            pr, pi = _cmul(qr, qi, jnp.where(keep, bre_ref[0, d, g2], 0.0),
                           jnp.where(keep, bim_ref[0, d, g2], 0.0))
            nr, ni = pr, pi
            fr = jnp.where(keep, cre_ref[0, d, g2], 0.0)
            fi = jnp.where(keep, cim_ref[0, d, g2], 0.0)
            gr, gi = fr, fi

            def rows(k):
                return slice(k * PAIR_LANES + g2 * GROUP_DIM, k * PAIR_LANES + (g2 + 1) * GROUP_DIM)

            for k in range(L + 1):
                if k < L:
                    if d == 0:
                        xr_ref[0, rows(k), :] = nr
                        xi_ref[0, rows(k), :] = ni
                        yr_ref[0, rows(k), :] = fr
                        yni_ref[0, rows(k), :] = -fi
                        wsi_ref[0, 0, rows(L - 1 - k), 0:LANES] = pr.astype(_BF16)
                        wsi_ref[0, 0, rows(L - 1 - k), LANES:2 * LANES] = pi.astype(_BF16)
                    else:
                        xr_ref[1, rows(k), :] = pr
                        xi_ref[1, rows(k), :] = pi
                        yr_ref[1, rows(k), :] = gr
                        yni_ref[1, rows(k), :] = -gi
                        wsi_ref[0, 0, rows(k), 2 * LANES:3 * LANES] = pr.astype(_BF16)
                        wsi_ref[0, 0, rows(k), 3 * LANES:4 * LANES] = pi.astype(_BF16)
                if k >= 1:
                    t = k - 1 if d == 0 else L - k
                    eo_ref[2 * d, rows(t), :] = fr
                    eo_ref[2 * d + 1, rows(t), :] = -fi
                pr, pi = _cmul(ar, ai, pr, pi)
                nr, ni = _cmul(ir, ii, nr, ni)
                fr, fi = _cmul(ar, ai, fr, fi)
                gr, gi = _cmul(ir, ii, gr, gi)

        a_r = jnp.where(lane1 < STATE, a_pair[0][0], a_pair[1][0])
        a_i = jnp.where(lane1 < STATE, a_pair[0][1], a_pair[1][1])
        alr, ali = a_r, a_i
        for _ in range(L - 1):
            alr, ali = _cmul(alr, ali, a_r, a_i)
        one, zero = jnp.ones((1, LANES), _F32), jnp.zeros((1, LANES), _F32)
        pw_ref[0, 0, d, 0, 0:1, :] = one
        pw_ref[0, 0, d, 1, 0:1, :] = zero

        def power(k, carry):
            wr, wi = _cmul(carry[0], carry[1], alr, ali)
            pw_ref[0, 0, d, 0, pl.ds(k + 1, 1), :] = wr
            pw_ref[0, 0, d, 1, pl.ds(k + 1, 1), :] = wi
            return wr, wi

        lax.fori_loop(0, nk, power, (one, zero))

    def split(v):
        hi = v.astype(_BF16)
        return hi, (v - hi.astype(_F32)).astype(_BF16)

    def lag_matrix(d):
        xh, xl = split(jnp.concatenate([xr_ref[d], xi_ref[d]], axis=1))
        yh, yl = split(jnp.concatenate([yr_ref[d], yni_ref[d]], axis=1))
        return lax.dot_general(jnp.concatenate([xh, xh, xl], axis=1), jnp.concatenate([yh, yl, yh], axis=1),
                               (((1,), (1,)), ((), ())), preferred_element_type=_F32)

    shift = PAIR_LANES.bit_length() - 1
    tau = lax.broadcasted_iota(jnp.int32, (PAIR_K, PAIR_K), 0) >> shift
    t = lax.broadcasted_iota(jnp.int32, (PAIR_K, PAIR_K), 1) >> shift
    t_ref[0, 0] = (jnp.where(t >= tau, lag_matrix(0), 0.0)
                   + jnp.where(tau >= t, lag_matrix(1), 0.0)).astype(_BF16)
    for j in range(4):
        wso_ref[0, 0, j * LANES:(j + 1) * LANES, :] = eo_ref[j].T.astype(_BF16)


def _s5_matrices(lam_re, lam_im, log_dt, b_re, b_im, c_re, c_im, nk):
    depth = lam_re.shape[0]
    dup = lambda v: jnp.concatenate([v, v], axis=-1)
    row = lambda v: dup(v)[:, :, :, None, :]
    ldt = jnp.broadcast_to(log_dt[..., None, None], (depth, 2, GROUPS, 1, LANES))
    bt = lambda v: dup(jnp.swapaxes(v, -1, -2))
    vec = pl.BlockSpec((1, 2, 2, 1, LANES), lambda l, q: (l, 0, q, 0, 0))
    mat = pl.BlockSpec((1, 2, 2, GROUP_DIM, LANES), lambda l, q: (l, 0, q, 0, 0))
    sq = pl.BlockSpec((1, 1, PAIR_K, PAIR_K), lambda l, q: (l, q, 0, 0))
    sq_shape = jax.ShapeDtypeStruct((depth, NPAIR, PAIR_K, PAIR_K), _BF16)
    return pl.pallas_call(
        functools.partial(_wprep_kernel, nk=nk),
        grid=(depth, NPAIR),
        in_specs=[vec, vec, vec, mat, mat, mat, mat],
        out_specs=[sq, sq, sq,
                   pl.BlockSpec((1, 1, 2, 2, nk + 1, LANES), lambda l, q: (l, q, 0, 0, 0, 0))],
        out_shape=[sq_shape, sq_shape, sq_shape,
                   jax.ShapeDtypeStruct((depth, NPAIR, 2, 2, nk + 1, LANES), _F32)],
        scratch_shapes=[pltpu.VMEM((2, PAIR_K, LANES), _F32)] * 4
                       + [pltpu.VMEM((4, PAIR_K, LANES), _F32)],
        compiler_params=_params(2),
        name="s5_weight_prep",
    )(row(lam_re), row(lam_im), ldt, bt(b_re), bt(b_im), dup(c_re), dup(c_im))


def _inproj_kernel(x_ref, npm_ref, win_ref, nv_ref, ws_ref, bs_ref, woa_ref,
                   ga_ref, u_ref, gb_ref, mix_ref, *, rows, tiles_per_seg, seg):
    d = D_MODEL
    nchunk = rows // GMLP_CHUNK
    h = _rms(x_ref[...], npm_ref[0]).astype(_BF16)
    vn = _rms(_gelu(_dot(h, win_ref[0, :, d:2 * d])), nv_ref[0]).astype(_BF16)
    r0 = pl.multiple_of((pl.program_id(0) % tiles_per_seg) * rows, rows)
    u_ref[pl.ds(r0, rows), :] = _dot(h, win_ref[0, :, 2 * d:3 * d])
    u_ref[seg:seg + SEG_PAD, :] = jnp.zeros((SEG_PAD, d), _F32)
    gb_ref[...] = _sigmoid(_dot(h, win_ref[0, :, 4 * d:5 * d])).astype(_BF16)
    ua = _gelu(_dot(h, win_ref[0, :, 0:d]))
    for g in range(A_GROUPS):
        cs = slice(g * LANES, (g + 1) * LANES)
        v_all = jnp.concatenate([vn[n * GMLP_CHUNK:(n + 1) * GMLP_CHUNK, cs] for n in range(nchunk)], axis=1)
        mixed = _dot(ws_ref[0, g], v_all)
        for n in range(nchunk):
            mix_ref[n * GMLP_CHUNK:(n + 1) * GMLP_CHUNK, cs] = (
                mixed[:, n * LANES:(n + 1) * LANES] + bs_ref[0, :, cs])
    z = (ua * mix_ref[...]).astype(_BF16)
    a = _dot(z, woa_ref[0])
    ga_ref[...] = (_sigmoid(_dot(h, win_ref[0, :, 3 * d:4 * d])) * a).astype(_BF16)


def _inproj(x, npm, w_in, nv, w_s, bias, w_out_a, layer, seg):
    ntok = x.shape[0]
    rows = min(TOK_ROWS, seg)
    tps = seg // rows
    pitch = seg + SEG_PAD
    tok = pl.BlockSpec((rows, D_MODEL), lambda i: (i, 0))
    return pl.pallas_call(
        functools.partial(_inproj_kernel, rows=rows, tiles_per_seg=tps, seg=seg),
        grid=(ntok // rows,),
        in_specs=[tok] + [_layer_spec(w.shape, layer) for w in (npm, w_in, nv, w_s, bias, w_out_a)],
        out_specs=[tok, pl.BlockSpec((pitch, D_MODEL), lambda i: (i // tps, 0)), tok],
        out_shape=[jax.ShapeDtypeStruct((ntok, D_MODEL), _BF16),
                   jax.ShapeDtypeStruct((ntok // seg * pitch, D_MODEL), _F32),
                   jax.ShapeDtypeStruct((ntok, D_MODEL), _BF16)],
        scratch_shapes=[pltpu.VMEM((rows, D_MODEL), _F32)],
        compiler_params=_params(1),
        name="inproj_gmlp",
    )(x, npm, w_in, nv, w_s, bias, w_out_a)


def _s5_kernel(u_ref, t_ref, wi_ref, wo_ref, pw_ref, skip_ref, perm_ref, y_ref,
               raw_ref, lhs_ref, z_ref, s_ref, yp_ref, yf_ref, *, nk, pitch):
    ns = BLK_STATE
    tgroups = CHUNK_L // PAIRS_PER_BLK
    tg_cols = PAIRS_PER_BLK * LANES

    def seg_rows(k, tau):
        return pl.ds(k * CHUNK_L + tau, NSEG, stride=pitch)

    def power_row(d, c, row):
        return jnp.concatenate([pw_ref[0, j, d, c, row, :] for j in range(PAIRS_PER_BLK)], axis=1)

    def gather(i, _):
        r = pl.ds(pl.multiple_of(i * ROW_BLK, ROW_BLK), ROW_BLK)
        for tau in range(CHUNK_L):
            v = jnp.concatenate([u_ref[seg_rows(i * KPB + kk, tau), :] for kk in range(KPB)], axis=0)
            raw_ref[r, tau * LANES:(tau + 1) * LANES] = v.astype(_BF16)
        return 0

    lax.fori_loop(0, nk // KPB, gather, 0, unroll=2)

    for tg in range(tgroups):
        pb = _dot(raw_ref[:, tg * tg_cols:(tg + 1) * tg_cols], perm_ref[...])
        for j in range(PAIRS_PER_BLK):
            lhs_ref[j, :, tg * LANES:(tg + 1) * LANES] = pb[:, j * LANES:(j + 1) * LANES].astype(_BF16)

    seg_id = lax.broadcasted_iota(jnp.int32, (NSEG, ns), 0)
    for d in range(2):
        for j in range(PAIRS_PER_BLK):
            zz = _dot(lhs_ref[j], wi_ref[0, j, :, d * 2 * LANES:(d + 1) * 2 * LANES])
            z_ref[d, :, j * LANES:(j + 1) * LANES] = zz[:, 0:LANES]
            z_ref[d, :, ns + j * LANES:ns + (j + 1) * LANES] = zz[:, LANES:2 * LANES]
    for j in range(PAIRS_PER_BLK):
        yp_ref[j] = _dot(lhs_ref[j], t_ref[0, j])

    for d in range(2):
        backward = d == 1
        al_r = jnp.broadcast_to(power_row(d, 0, slice(1, 2)), (NSEG, ns))
        al_i = jnp.broadcast_to(power_row(d, 1, slice(1, 2)), (NSEG, ns))

        def local_scan(j, carry):
            k = nk - 1 - j if backward else j
            r = pl.ds(pl.multiple_of(k * SUBLANES, SUBLANES), SUBLANES)
            sr, si = carry
            zr, zi = z_ref[d, r, 0:ns], z_ref[d, r, ns:2 * ns]
            z_ref[d, r, 0:ns] = sr
            z_ref[d, r, ns:2 * ns] = si
            pr, pi = _cmul(al_r, al_i, sr, si)
            return pr + zr, pi + zi

        zero = jnp.zeros((NSEG, ns), _F32)
        tot_r, tot_i = lax.fori_loop(0, nk, local_scan, (zero, zero), unroll=SCAN_UNROLL)

        as_r = jnp.broadcast_to(power_row(d, 0, slice(nk, nk + 1)), (NSEG, ns))
        as_i = jnp.broadcast_to(power_row(d, 1, slice(nk, nk + 1)), (NSEG, ns))
        edge = NSEG - 1 if backward else 0
        shift = NSEG - 1 if backward else 1
        cr, ci = zero, zero
        for _ in range(NSEG - 1):
            pr, pi = _cmul(as_r, as_i, cr, ci)
            cr = jnp.where(seg_id == edge, 0.0, pltpu.roll(pr + tot_r, shift, 0))
            ci = jnp.where(seg_id == edge, 0.0, pltpu.roll(pi + tot_i, shift, 0))

        def add_carry(i, _):
            r = pl.ds(pl.multiple_of(i * ROW_BLK, ROW_BLK), ROW_BLK)
            sr, si = [], []
            for kk in range(KPB):
                k = i * KPB + kk
                e = nk - 1 - k if backward else k
                pr, pi = _cmul(power_row(d, 0, pl.ds(e, 1)), power_row(d, 1, pl.ds(e, 1)), cr, ci)
                rk = pl.ds(pl.multiple_of(k * SUBLANES, SUBLANES), SUBLANES)
                sr.append(z_ref[d, rk, 0:ns] + pr)
                si.append(z_ref[d, rk, ns:2 * ns] + pi)
            s_ref[r, d * 2 * ns:d * 2 * ns + ns] = jnp.concatenate(sr, axis=0).astype(_BF16)
            s_ref[r, d * 2 * ns + ns:(d + 1) * 2 * ns] = jnp.concatenate(si, axis=0).astype(_BF16)
            return 0

        lax.fori_loop(0, nk // KPB, add_carry, 0, unroll=SCAN_UNROLL)

    for j in range(PAIRS_PER_BLK):
        sp = jnp.concatenate([s_ref[:, q * ns + j * LANES:q * ns + (j + 1) * LANES] for q in range(4)],
                             axis=1)
        yp_ref[j] = yp_ref[j] + _dot(sp, wo_ref[0, j])

    for tg in range(tgroups):
        yc = jnp.concatenate([yp_ref[j, :, tg * LANES:(tg + 1) * LANES] for j in range(PAIRS_PER_BLK)],
                             axis=1)
        yf_ref[:, tg * tg_cols:(tg + 1) * tg_cols] = _dot(yc.astype(_BF16), perm_ref[...])

    skip = skip_ref[0, 0]

    def scatter(k, _):
        r = pl.ds(pl.multiple_of(k * SUBLANES, SUBLANES), SUBLANES)
        for t in range(CHUNK_L):
            rows = seg_rows(k, t)
            y_ref[rows, :] = yf_ref[r, t * LANES:(t + 1) * LANES] + skip * u_ref[rows, :]
        return 0

    lax.fori_loop(0, nk, scatter, 0, unroll=2)
    seg = nk * CHUNK_L
    for s in range(NSEG):
        y_ref[s * pitch + seg:(s + 1) * pitch, :] = jnp.zeros((SEG_PAD, LANES), _F32)


def _pair_permutation():
    n = PAIRS_PER_BLK
    src = jnp.arange(n * LANES)
    tl, j, c = src // LANES, (src // PAIR_LANES) % n, src % PAIR_LANES
    dst = j * LANES + tl * PAIR_LANES + c
    return (dst[:, None] == jnp.arange(n * LANES)[None, :]).astype(_BF16)


def _s5_core(u_pad, t_mat, w_si, w_so, pw, skip, layer, nseq, seg):
    nk = seg // CHUNK_L
    pitch = seg + SEG_PAD
    m = nk * NSEG
    perm = _pair_permutation()
    seq_blk = pl.BlockSpec((NSEG * pitch, LANES), lambda b, s: (s, b))
    sq = pl.BlockSpec((1, PAIRS_PER_BLK, PAIR_K, PAIR_K), lambda b, s: (layer, b, 0, 0))
    return pl.pallas_call(
        functools.partial(_s5_kernel, nk=nk, pitch=pitch),
        grid=(NBLK, nseq),
        in_specs=[seq_blk, sq, sq, sq,
                  pl.BlockSpec((1, PAIRS_PER_BLK, 2, 2, nk + 1, LANES), lambda b, s: (layer, b, 0, 0, 0, 0)),
                  pl.BlockSpec((1, 1, 1, LANES), lambda b, s: (layer, b, 0, 0)),
                  pl.BlockSpec(perm.shape, lambda b, s: (0, 0))],
        out_specs=seq_blk,
        out_shape=jax.ShapeDtypeStruct(u_pad.shape, _F32),
        scratch_shapes=[pltpu.VMEM((m, CHUNK_L * LANES), _BF16),
                        pltpu.VMEM((PAIRS_PER_BLK, m, PAIR_K), _BF16),
                        pltpu.VMEM((2, m, 2 * BLK_STATE), _F32),
                        pltpu.VMEM((m, 4 * BLK_STATE), _BF16),
                        pltpu.VMEM((PAIRS_PER_BLK, m, PAIR_K), _F32),
                        pltpu.VMEM((m, CHUNK_L * LANES), _F32)],
        compiler_params=_params(2),
        name="s5_core",
    )(u_pad, t_mat, w_si, w_so, pw, skip, perm)


def _merge_kernel(y_ref, ga_ref, gb_ref, x_ref, wglu_ref, wo_ref, npost_ref, o_ref, *, rows, tiles_per_seg):
    d = D_MODEL
    r0 = pl.multiple_of((pl.program_id(0) % tiles_per_seg) * rows, rows)
    parts = [slice(q * (rows // ROW_PARTS), (q + 1) * (rows // ROW_PARTS)) for q in range(ROW_PARTS)]
    z = [_gelu(y_ref[pl.ds(r0 + p.start, p.stop - p.start), :]).astype(_BF16) for p in parts]
    g1 = [_dot(zq, wglu_ref[0, :, 0:d]) for zq in z]
    g2 = [_dot(zq, wglu_ref[0, :, d:2 * d]) for zq in z]
    m = [(ga_ref[p, :].astype(_F32) + gb_ref[p, :].astype(_F32) * (a * _sigmoid(b))).astype(_BF16)
         for p, a, b in zip(parts, g1, g2)]
    for p, mq in zip(parts, m):
        o_ref[p, :] = x_ref[p, :] + _rms(_dot(mq, wo_ref[0]), npost_ref[0])


def _merge(y_pad, ga, gb, x, w_glu, w_o, npost, layer, seg):
    ntok = x.shape[0]
    rows = min(TOK_ROWS, seg)
    tps = seg // rows
    pitch = seg + SEG_PAD
    tok = pl.BlockSpec((rows, D_MODEL), lambda i: (i, 0))
    return pl.pallas_call(
        functools.partial(_merge_kernel, rows=rows, tiles_per_seg=tps),
        grid=(ntok // rows,),
        in_specs=[pl.BlockSpec((pitch, D_MODEL), lambda i: (i // tps, 0)), tok, tok, tok]
                 + [_layer_spec(w.shape, layer) for w in (w_glu, w_o, npost)],
        out_specs=tok,
        out_shape=jax.ShapeDtypeStruct((ntok, D_MODEL), _F32),
        compiler_params=_params(1),
        name="glu_merge",
    )(y_pad, ga, gb, x, w_glu, w_o, npost)


def _ffn_kernel(x_ref, npre_ref, w1_ref, w2_ref, npost_ref, o_ref):
    x = x_ref[...]
    h = _rms(x, npre_ref[0]).astype(_BF16)
    f = jnp.zeros(x.shape, _F32)
    for j in range(D_FF // D_MODEL):
        cs = slice(j * D_MODEL, (j + 1) * D_MODEL)
        r = jnp.maximum(_dot(h, w1_ref[0, :, cs]), 0.0)
        f = f + _dot((r * r).astype(_BF16), w2_ref[0, cs, :])
    o_ref[...] = x + _rms(f, npost_ref[0])


def _ffn(x, npre, w1, w2, npost, layer, rows):
    ntok = x.shape[0]
    tok = pl.BlockSpec((rows, D_MODEL), lambda i: (i, 0))
    return pl.pallas_call(
        _ffn_kernel,
        grid=(ntok // rows,),
        in_specs=[tok] + [_layer_spec(w.shape, layer) for w in (npre, w1, w2, npost)],
        out_specs=tok,
        out_shape=jax.ShapeDtypeStruct((ntok, D_MODEL), _F32),
        compiler_params=_params(1),
        name="ffn",
    )(x, npre, w1, w2, npost)


def _trunk(x, p, s5m):
    nseq, seqlen, d = x.shape
    seg = seqlen // NSEG
    assert d == D_MODEL and seg % GMLP_CHUNK == 0 and seqlen % TOK_ROWS == 0
    x = x.reshape(nseq * seqlen, d)
    t_mat, w_si, w_so, pw = s5m
    for l in range(p["w_in"].shape[0]):
        ga, u_pad, gb = _inproj(x, p["norm_pre_mix"], p["w_in"], p["norm_v"], p["w_s"], p["bias"],
                                p["w_out_a"], l, seg)
        y_pad = _s5_core(u_pad, t_mat, w_si, w_so, pw, p["d_skip"], l, nseq, seg)
        x = _merge(y_pad, ga, gb, x, p["w_glu"], p["w_o"], p["norm_post_mix"], l, seg)
        x = _ffn(x, p["norm_pre_ff"], p["w_ff1"], p["w_ff2"], p["norm_post_ff"], l, min(FF_ROWS, seqlen))
    return x.reshape(nseq, seqlen, d)


def kernel(x_prompt, x_sample, norm_pre_mix, w_in, norm_v, w_s, b_s, w_out_a, lam_re, lam_im,
           log_dt, b_re, b_im, c_re, c_im, d_skip, w_glu, w_o, norm_post_mix, norm_pre_ff,
           w_ff1, w_ff2, norm_post_ff):
    depth = w_in.shape[0]
    bf = lambda w: w.astype(_BF16)
    row = lambda v: v.reshape(depth, 1, D_MODEL)
    p = {
        "norm_pre_mix": row(norm_pre_mix), "w_in": bf(w_in), "norm_v": row(norm_v), "w_s": bf(w_s),
        "bias": jnp.repeat(jnp.swapaxes(b_s, 1, 2), LANES, axis=2),
        "w_out_a": bf(w_out_a), "d_skip": d_skip.reshape(depth, NBLK, 1, LANES),
        "w_glu": bf(w_glu), "w_o": bf(w_o), "norm_post_mix": row(norm_post_mix),
        "norm_pre_ff": row(norm_pre_ff), "w_ff1": bf(w_ff1), "w_ff2": bf(w_ff2),
        "norm_post_ff": row(norm_post_ff),
    }
    outs = []
    mats = {}
    for x in (x_prompt, x_sample):
        nk = x.shape[1] // (NSEG * CHUNK_L)
        if nk not in mats:
            mats[nk] = _s5_matrices(lam_re, lam_im, log_dt, b_re, b_im, c_re, c_im, nk)
        outs.append(_trunk(x, p, mats[nk]))
    return tuple(outs)
```

```python
import functools

import jax
import jax.numpy as jnp
from jax import lax
from jax.experimental import pallas as pl
from jax.experimental.pallas import tpu as pltpu

D_MODEL = 1024
GMLP_CHUNK = 128
A_GROUPS = 8
GROUPS = 64
GROUP_DIM = 16
STATE = 64
D_FF = 4 * D_MODEL
EPS = 1e-6

LANES = 128
SUBLANES = 8
CHUNK_L = 16
NSEG = SUBLANES
NBLK = D_MODEL // LANES
PAIR_LANES = 2 * GROUP_DIM
PAIRS_PER_BLK = LANES // PAIR_LANES
NPAIR = GROUPS // 2
PAIR_K = CHUNK_L * PAIR_LANES
PAIR_STATE = 2 * STATE
BLK_STATE = PAIRS_PER_BLK * PAIR_STATE
SEG_PAD = SUBLANES
ROW_BLK = 2 * SUBLANES
KPB = ROW_BLK // SUBLANES
SCAN_UNROLL = True
TOK_ROWS = 512
ROW_PARTS = 2
FF_ROWS = 1024
VMEM_LIMIT = 56 * 1024 * 1024

_F32 = jnp.float32
_BF16 = jnp.bfloat16


def _dot(a, b):
    return jnp.dot(a, b, preferred_element_type=_F32)


def _rms(x, g):
    return x * lax.rsqrt(jnp.mean(x * x, axis=-1, keepdims=True) + EPS) * g


def _sigmoid(x):
    return 1.0 / (1.0 + jnp.exp(-x))


def _gelu(x):
    return jax.nn.gelu(x, approximate=True)


def _layer_spec(shape, layer):
    nd = len(shape)
    return pl.BlockSpec((1,) + tuple(shape[1:]), lambda *_: (layer,) + (0,) * (nd - 1),
                        pipeline_mode=pl.Buffered(1))


def _params(n_axes):
    return pltpu.CompilerParams(
        dimension_semantics=("arbitrary",) * n_axes, vmem_limit_bytes=VMEM_LIMIT)


def _cmul(ar, ai, br, bi):
    return ar * br - ai * bi, ar * bi + ai * br


def _wprep_kernel(lre_ref, lim_ref, ldt_ref, bre_ref, bim_ref, cre_ref, cim_ref,
                  t_ref, wsi_ref, wso_ref, pw_ref,
                  xr_ref, xi_ref, yr_ref, yni_ref, eo_ref, *, nk):
    L = CHUNK_L
    lane = lax.broadcasted_iota(jnp.int32, (GROUP_DIM, LANES), 1)
    lane1 = lax.broadcasted_iota(jnp.int32, (1, LANES), 1)
    for d in range(2):
        a_pair = []
        for g2 in range(2):
            lre, lim = lre_ref[0, d, g2], lim_ref[0, d, g2]
            dt = jnp.exp(ldt_ref[0, d, g2])
            mag = jnp.exp(lre * dt)
            ar, ai = mag * jnp.cos(lim * dt), mag * jnp.sin(lim * dt)
            a_pair.append((ar, ai))
            den = lre * lre + lim * lim
            qr = ((ar - 1.0) * lre + ai * lim) / den
            qi = (ai * lre - (ar - 1.0) * lim) / den
            n2 = ar * ar + ai * ai
            ir, ii = ar / n2, -ai / n2
            keep = (lane >= g2 * STATE) & (lane < (g2 + 1) * STATE)
            pr, pi = _cmul(qr, qi, jnp.where(keep, bre_ref[0, d, g2], 0.0),
                           jnp.where(keep, bim_ref[0, d, g2], 0.0))
            nr, ni = pr, pi
            fr = jnp.where(keep, cre_ref[0, d, g2], 0.0)
            fi = jnp.where(keep, cim_ref[0, d, g2], 0.0)
            gr, gi = fr, fi

            def rows(k):
                return slice(k * PAIR_LANES + g2 * GROUP_DIM, k * PAIR_LANES + (g2 + 1) * GROUP_DIM)

            for k in range(L + 1):
                if k < L:
                    if d == 0:
                        xr_ref[0, rows(k), :] = nr
                        xi_ref[0, rows(k), :] = ni
                        yr_ref[0, rows(k), :] = fr
                        yni_ref[0, rows(k), :] = -fi
                        wsi_ref[0, 0, rows(L - 1 - k), 0:LANES] = pr.astype(_BF16)
                        wsi_ref[0, 0, rows(L - 1 - k), LANES:2 * LANES] = pi.astype(_BF16)
                    else:
                        xr_ref[1, rows(k), :] = pr
                        xi_ref[1, rows(k), :] = pi
                        yr_ref[1, rows(k), :] = gr
                        yni_ref[1, rows(k), :] = -gi
                        wsi_ref[0, 0, rows(k), 2 * LANES:3 * LANES] = pr.astype(_BF16)
                        wsi_ref[0, 0, rows(k), 3 * LANES:4 * LANES] = pi.astype(_BF16)
                if k >= 1:
                    t = k - 1 if d == 0 else L - k
                    eo_ref[2 * d, rows(t), :] = fr
                    eo_ref[2 * d + 1, rows(t), :] = -fi
                pr, pi = _cmul(ar, ai, pr, pi)
                nr, ni = _cmul(ir, ii, nr, ni)
                fr, fi = _cmul(ar, ai, fr, fi)
                gr, gi = _cmul(ir, ii, gr, gi)

        a_r = jnp.where(lane1 < STATE, a_pair[0][0], a_pair[1][0])
        a_i = jnp.where(lane1 < STATE, a_pair[0][1], a_pair[1][1])
        alr, ali = a_r, a_i
        for _ in range(L - 1):
            alr, ali = _cmul(alr, ali, a_r, a_i)
        one, zero = jnp.ones((1, LANES), _F32), jnp.zeros((1, LANES), _F32)
        pw_ref[0, 0, d, 0, 0:1, :] = one
        pw_ref[0, 0, d, 1, 0:1, :] = zero

        def power(k, carry):
            wr, wi = _cmul(carry[0], carry[1], alr, ali)
            pw_ref[0, 0, d, 0, pl.ds(k + 1, 1), :] = wr
            pw_ref[0, 0, d, 1, pl.ds(k + 1, 1), :] = wi
            return wr, wi

        lax.fori_loop(0, nk, power, (one, zero))

    def split(v):
        hi = v.astype(_BF16)
        return hi, (v - hi.astype(_F32)).astype(_BF16)

    def lag_matrix(d):
        xh, xl = split(jnp.concatenate([xr_ref[d], xi_ref[d]], axis=1))
        yh, yl = split(jnp.concatenate([yr_ref[d], yni_ref[d]], axis=1))
        return lax.dot_general(jnp.concatenate([xh, xh, xl], axis=1), jnp.concatenate([yh, yl, yh], axis=1),
                               (((1,), (1,)), ((), ())), preferred_element_type=_F32)

    shift = PAIR_LANES.bit_length() - 1
    tau = lax.broadcasted_iota(jnp.int32, (PAIR_K, PAIR_K), 0) >> shift
    t = lax.broadcasted_iota(jnp.int32, (PAIR_K, PAIR_K), 1) >> shift
    t_ref[0, 0] = (jnp.where(t >= tau, lag_matrix(0), 0.0)
                   + jnp.where(tau >= t, lag_matrix(1), 0.0)).astype(_BF16)
    for j in range(4):
        wso_ref[0, 0, j * LANES:(j + 1) * LANES, :] = eo_ref[j].T.astype(_BF16)


def _s5_matrices(lam_re, lam_im, log_dt, b_re, b_im, c_re, c_im, nk):
    depth = lam_re.shape[0]
    dup = lambda v: jnp.concatenate([v, v], axis=-1)
    row = lambda v: dup(v)[:, :, :, None, :]
    ldt = jnp.broadcast_to(log_dt[..., None, None], (depth, 2, GROUPS, 1, LANES))
    bt = lambda v: dup(jnp.swapaxes(v, -1, -2))
    vec = pl.BlockSpec((1, 2, 2, 1, LANES), lambda l, q: (l, 0, q, 0, 0))
    mat = pl.BlockSpec((1, 2, 2, GROUP_DIM, LANES), lambda l, q: (l, 0, q, 0, 0))
    sq = pl.BlockSpec((1, 1, PAIR_K, PAIR_K), lambda l, q: (l, q, 0, 0))
    sq_shape = jax.ShapeDtypeStruct((depth, NPAIR, PAIR_K, PAIR_K), _BF16)
    return pl.pallas_call(
        functools.partial(_wprep_kernel, nk=nk),
        grid=(depth, NPAIR),
        in_specs=[vec, vec, vec, mat, mat, mat, mat],
        out_specs=[sq, sq, sq,
                   pl.BlockSpec((1, 1, 2, 2, nk + 1, LANES), lambda l, q: (l, q, 0, 0, 0, 0))],
        out_shape=[sq_shape, sq_shape, sq_shape,
                   jax.ShapeDtypeStruct((depth, NPAIR, 2, 2, nk + 1, LANES), _F32)],
        scratch_shapes=[pltpu.VMEM((2, PAIR_K, LANES), _F32)] * 4
                       + [pltpu.VMEM((4, PAIR_K, LANES), _F32)],
        compiler_params=_params(2),
        name="s5_weight_prep",
    )(row(lam_re), row(lam_im), ldt, bt(b_re), bt(b_im), dup(c_re), dup(c_im))


def _inproj_kernel(x_ref, npm_ref, win_ref, nv_ref, ws_ref, bs_ref, woa_ref,
                   ga_ref, u_ref, gb_ref, mix_ref, *, rows, tiles_per_seg, seg):
    d = D_MODEL
    prow = rows // ROW_PARTS
    nchunk = prow // GMLP_CHUNK
    parts = [slice(q * prow, (q + 1) * prow) for q in range(ROW_PARTS)]
    h = [_rms(x_ref[p, :], npm_ref[0]).astype(_BF16) for p in parts]
    vn = [_rms(_gelu(_dot(hq, win_ref[0, :, d:2 * d])), nv_ref[0]).astype(_BF16) for hq in h]
    r0 = pl.multiple_of((pl.program_id(0) % tiles_per_seg) * rows, rows)
    for p, hq in zip(parts, h):
        u_ref[pl.ds(r0 + p.start, prow), :] = _dot(hq, win_ref[0, :, 2 * d:3 * d])
        gb_ref[p, :] = _sigmoid(_dot(hq, win_ref[0, :, 4 * d:5 * d])).astype(_BF16)
    u_ref[seg:seg + SEG_PAD, :] = jnp.zeros((SEG_PAD, d), _F32)
    ua = [_gelu(_dot(hq, win_ref[0, :, 0:d])) for hq in h]
    for p, vq in zip(parts, vn):
        for g in range(A_GROUPS):
            cs = slice(g * LANES, (g + 1) * LANES)
            v_all = jnp.concatenate([vq[n * GMLP_CHUNK:(n + 1) * GMLP_CHUNK, cs] for n in range(nchunk)], axis=1)
            mixed = _dot(ws_ref[0, g], v_all)
            for n in range(nchunk):
                mix_ref[p.start + n * GMLP_CHUNK:p.start + (n + 1) * GMLP_CHUNK, cs] = (
                    mixed[:, n * LANES:(n + 1) * LANES] + bs_ref[0, :, cs])
    for p, hq, uq in zip(parts, h, ua):
        a = _dot((uq * mix_ref[p, :]).astype(_BF16), woa_ref[0])
        ga_ref[p, :] = (_sigmoid(_dot(hq, win_ref[0, :, 3 * d:4 * d])) * a).astype(_BF16)


def _inproj(x, npm, w_in, nv, w_s, bias, w_out_a, layer, seg):
    ntok = x.shape[0]
    rows = min(TOK_ROWS, seg)
    tps = seg // rows
    pitch = seg + SEG_PAD
    tok = pl.BlockSpec((rows, D_MODEL), lambda i: (i, 0))
    return pl.pallas_call(
        functools.partial(_inproj_kernel, rows=rows, tiles_per_seg=tps, seg=seg),
        grid=(ntok // rows,),
        in_specs=[tok] + [_layer_spec(w.shape, layer) for w in (npm, w_in, nv, w_s, bias, w_out_a)],
        out_specs=[tok, pl.BlockSpec((pitch, D_MODEL), lambda i: (i // tps, 0)), tok],
        out_shape=[jax.ShapeDtypeStruct((ntok, D_MODEL), _BF16),
                   jax.ShapeDtypeStruct((ntok // seg * pitch, D_MODEL), _F32),
                   jax.ShapeDtypeStruct((ntok, D_MODEL), _BF16)],
        scratch_shapes=[pltpu.VMEM((rows, D_MODEL), _F32)],
        compiler_params=_params(1),
        name="inproj_gmlp",
    )(x, npm, w_in, nv, w_s, bias, w_out_a)


def _s5_kernel(u_ref, t_ref, wi_ref, wo_ref, pw_ref, skip_ref, perm_ref, y_ref,
               raw_ref, lhs_ref, z_ref, s_ref, yp_ref, yf_ref, *, nk, pitch):
    ns = BLK_STATE
    tgroups = CHUNK_L // PAIRS_PER_BLK
    tg_cols = PAIRS_PER_BLK * LANES

    def seg_rows(k, tau):
        return pl.ds(k * CHUNK_L + tau, NSEG, stride=pitch)

    def power_row(d, c, row):
        return jnp.concatenate([pw_ref[0, j, d, c, row, :] for j in range(PAIRS_PER_BLK)], axis=1)

    def gather(tg):
        def body(i, _):
            r = pl.ds(pl.multiple_of(i * ROW_BLK, ROW_BLK), ROW_BLK)
            for tau in range(tg * PAIRS_PER_BLK, (tg + 1) * PAIRS_PER_BLK):
                v = jnp.concatenate([u_ref[seg_rows(i * KPB + kk, tau), :] for kk in range(KPB)], axis=0)
                raw_ref[r, tau * LANES:(tau + 1) * LANES] = v.astype(_BF16)
            return 0

        lax.fori_loop(0, nk // KPB, body, 0, unroll=True)

    for tg in range(tgroups):
        gather(tg)
        pb = _dot(raw_ref[:, tg * tg_cols:(tg + 1) * tg_cols], perm_ref[...])
        for j in range(PAIRS_PER_BLK):
            lhs_ref[j, :, tg * LANES:(tg + 1) * LANES] = pb[:, j * LANES:(j + 1) * LANES].astype(_BF16)

    seg_id = lax.broadcasted_iota(jnp.int32, (NSEG, ns), 0)
    for d in range(2):
        for j in range(PAIRS_PER_BLK):
            zz = _dot(lhs_ref[j], wi_ref[0, j, :, d * 2 * LANES:(d + 1) * 2 * LANES])
            z_ref[d, :, j * LANES:(j + 1) * LANES] = zz[:, 0:LANES]
            z_ref[d, :, ns + j * LANES:ns + (j + 1) * LANES] = zz[:, LANES:2 * LANES]
    for j in range(PAIRS_PER_BLK):
        yp_ref[j] = _dot(lhs_ref[j], t_ref[0, j])

    for d in range(2):
        backward = d == 1
        al_r = jnp.broadcast_to(power_row(d, 0, slice(1, 2)), (NSEG, ns))
        al_i = jnp.broadcast_to(power_row(d, 1, slice(1, 2)), (NSEG, ns))

        def local_scan(j, carry):
            k = nk - 1 - j if backward else j
            r = pl.ds(pl.multiple_of(k * SUBLANES, SUBLANES), SUBLANES)
            sr, si = carry
            zr, zi = z_ref[d, r, 0:ns], z_ref[d, r, ns:2 * ns]
            z_ref[d, r, 0:ns] = sr
            z_ref[d, r, ns:2 * ns] = si
            pr, pi = _cmul(al_r, al_i, sr, si)
            return pr + zr, pi + zi

        zero = jnp.zeros((NSEG, ns), _F32)
        tot_r, tot_i = lax.fori_loop(0, nk, local_scan, (zero, zero), unroll=SCAN_UNROLL)

        as_r = jnp.broadcast_to(power_row(d, 0, slice(nk, nk + 1)), (NSEG, ns))
        as_i = jnp.broadcast_to(power_row(d, 1, slice(nk, nk + 1)), (NSEG, ns))
        edge = NSEG - 1 if backward else 0
        shift = NSEG - 1 if backward else 1
        cr, ci = zero, zero
        for _ in range(NSEG - 1):
            pr, pi = _cmul(as_r, as_i, cr, ci)
            cr = jnp.where(seg_id == edge, 0.0, pltpu.roll(pr + tot_r, shift, 0))
            ci = jnp.where(seg_id == edge, 0.0, pltpu.roll(pi + tot_i, shift, 0))

        def add_carry(i, _):
            r = pl.ds(pl.multiple_of(i * ROW_BLK, ROW_BLK), ROW_BLK)
            sr, si = [], []
            for kk in range(KPB):
                k = i * KPB + kk
                e = nk - 1 - k if backward else k
                pr, pi = _cmul(power_row(d, 0, pl.ds(e, 1)), power_row(d, 1, pl.ds(e, 1)), cr, ci)
                rk = pl.ds(pl.multiple_of(k * SUBLANES, SUBLANES), SUBLANES)
                sr.append(z_ref[d, rk, 0:ns] + pr)
                si.append(z_ref[d, rk, ns:2 * ns] + pi)
            s_ref[r, d * 2 * ns:d * 2 * ns + ns] = jnp.concatenate(sr, axis=0).astype(_BF16)
            s_ref[r, d * 2 * ns + ns:(d + 1) * 2 * ns] = jnp.concatenate(si, axis=0).astype(_BF16)
            return 0

        lax.fori_loop(0, nk // KPB, add_carry, 0, unroll=SCAN_UNROLL)

    for j in range(PAIRS_PER_BLK):
        sp = jnp.concatenate([s_ref[:, q * ns + j * LANES:q * ns + (j + 1) * LANES] for q in range(4)],
                             axis=1)
        yp_ref[j] = yp_ref[j] + _dot(sp, wo_ref[0, j])

    skip = skip_ref[0, 0]

    def scatter(tg):
        def body(k, _):
            r = pl.ds(pl.multiple_of(k * SUBLANES, SUBLANES), SUBLANES)
            for t in range(tg * PAIRS_PER_BLK, (tg + 1) * PAIRS_PER_BLK):
                rows = seg_rows(k, t)
                y_ref[rows, :] = yf_ref[r, t * LANES:(t + 1) * LANES] + skip * u_ref[rows, :]
            return 0

        lax.fori_loop(0, nk, body, 0, unroll=True)

    for tg in range(tgroups):
        yc = jnp.concatenate([yp_ref[j, :, tg * LANES:(tg + 1) * LANES] for j in range(PAIRS_PER_BLK)],
                             axis=1)
        yf_ref[:, tg * tg_cols:(tg + 1) * tg_cols] = _dot(yc.astype(_BF16), perm_ref[...])
        scatter(tg)
    seg = nk * CHUNK_L
    for s in range(NSEG):
        y_ref[s * pitch + seg:(s + 1) * pitch, :] = jnp.zeros((SEG_PAD, LANES), _F32)


def _pair_permutation():
    n = PAIRS_PER_BLK
    src = jnp.arange(n * LANES)
    tl, j, c = src // LANES, (src // PAIR_LANES) % n, src % PAIR_LANES
    dst = j * LANES + tl * PAIR_LANES + c
    return (dst[:, None] == jnp.arange(n * LANES)[None, :]).astype(_BF16)


def _s5_core(u_pad, t_mat, w_si, w_so, pw, skip, layer, nseq, seg):
    nk = seg // CHUNK_L
    pitch = seg + SEG_PAD
    m = nk * NSEG
    perm = _pair_permutation()
    seq_blk = pl.BlockSpec((NSEG * pitch, LANES), lambda b, s: (s, b))
    sq = pl.BlockSpec((1, PAIRS_PER_BLK, PAIR_K, PAIR_K), lambda b, s: (layer, b, 0, 0))
    return pl.pallas_call(
        functools.partial(_s5_kernel, nk=nk, pitch=pitch),
        grid=(NBLK, nseq),
        in_specs=[seq_blk, sq, sq, sq,
                  pl.BlockSpec((1, PAIRS_PER_BLK, 2, 2, nk + 1, LANES), lambda b, s: (layer, b, 0, 0, 0, 0)),
                  pl.BlockSpec((1, 1, 1, LANES), lambda b, s: (layer, b, 0, 0)),
                  pl.BlockSpec(perm.shape, lambda b, s: (0, 0))],
        out_specs=seq_blk,
        out_shape=jax.ShapeDtypeStruct(u_pad.shape, _F32),
        scratch_shapes=[pltpu.VMEM((m, CHUNK_L * LANES), _BF16),
                        pltpu.VMEM((PAIRS_PER_BLK, m, PAIR_K), _BF16),
                        pltpu.VMEM((2, m, 2 * BLK_STATE), _F32),
                        pltpu.VMEM((m, 4 * BLK_STATE), _BF16),
                        pltpu.VMEM((PAIRS_PER_BLK, m, PAIR_K), _F32),
                        pltpu.VMEM((m, CHUNK_L * LANES), _F32)],
        compiler_params=_params(2),
        name="s5_core",
    )(u_pad, t_mat, w_si, w_so, pw, skip, perm)


def _merge_kernel(y_ref, ga_ref, gb_ref, x_ref, wglu_ref, wo_ref, npost_ref, o_ref, *, rows, tiles_per_seg):
    d = D_MODEL
    r0 = pl.multiple_of((pl.program_id(0) % tiles_per_seg) * rows, rows)
    parts = [slice(q * (rows // ROW_PARTS), (q + 1) * (rows // ROW_PARTS)) for q in range(ROW_PARTS)]
    z = [_gelu(y_ref[pl.ds(r0 + p.start, p.stop - p.start), :]).astype(_BF16) for p in parts]
    g1 = [_dot(zq, wglu_ref[0, :, 0:d]) for zq in z]
    g2 = [_dot(zq, wglu_ref[0, :, d:2 * d]) for zq in z]
    m = [(ga_ref[p, :].astype(_F32) + gb_ref[p, :].astype(_F32) * (a * _sigmoid(b))).astype(_BF16)
         for p, a, b in zip(parts, g1, g2)]
    for p, mq in zip(parts, m):
        o_ref[p, :] = x_ref[p, :] + _rms(_dot(mq, wo_ref[0]), npost_ref[0])


def _merge(y_pad, ga, gb, x, w_glu, w_o, npost, layer, seg):
    ntok = x.shape[0]
    rows = min(TOK_ROWS, seg)
    tps = seg // rows
    pitch = seg + SEG_PAD
    tok = pl.BlockSpec((rows, D_MODEL), lambda i: (i, 0))
    return pl.pallas_call(
        functools.partial(_merge_kernel, rows=rows, tiles_per_seg=tps),
        grid=(ntok // rows,),
        in_specs=[pl.BlockSpec((pitch, D_MODEL), lambda i: (i // tps, 0)), tok, tok, tok]
                 + [_layer_spec(w.shape, layer) for w in (w_glu, w_o, npost)],
        out_specs=tok,
        out_shape=jax.ShapeDtypeStruct((ntok, D_MODEL), _F32),
        compiler_params=_params(1),
        name="glu_merge",
    )(y_pad, ga, gb, x, w_glu, w_o, npost)


def _ffn_kernel(x_ref, npre_ref, w1_ref, w2_ref, npost_ref, o_ref):
    x = x_ref[...]
    h = _rms(x, npre_ref[0]).astype(_BF16)
    f = jnp.zeros(x.shape, _F32)
    for j in range(D_FF // D_MODEL):
        cs = slice(j * D_MODEL, (j + 1) * D_MODEL)
        r = jnp.maximum(_dot(h, w1_ref[0, :, cs]), 0.0)
        f = f + _dot((r * r).astype(_BF16), w2_ref[0, cs, :])
    o_ref[...] = x + _rms(f, npost_ref[0])


def _ffn(x, npre, w1, w2, npost, layer, rows):
    ntok = x.shape[0]
    tok = pl.BlockSpec((rows, D_MODEL), lambda i: (i, 0))
    return pl.pallas_call(
        _ffn_kernel,
        grid=(ntok // rows,),
        in_specs=[tok] + [_layer_spec(w.shape, layer) for w in (npre, w1, w2, npost)],
        out_specs=tok,
        out_shape=jax.ShapeDtypeStruct((ntok, D_MODEL), _F32),
        compiler_params=_params(1),
        name="ffn",
    )(x, npre, w1, w2, npost)


def _trunk(x, p, s5m):
    nseq, seqlen, d = x.shape
    seg = seqlen // NSEG
    assert d == D_MODEL and seg % GMLP_CHUNK == 0 and seqlen % TOK_ROWS == 0
    x = x.reshape(nseq * seqlen, d)
    t_mat, w_si, w_so, pw = s5m
    for l in range(p["w_in"].shape[0]):
        ga, u_pad, gb = _inproj(x, p["norm_pre_mix"], p["w_in"], p["norm_v"], p["w_s"], p["bias"],
                                p["w_out_a"], l, seg)
        y_pad = _s5_core(u_pad, t_mat, w_si, w_so, pw, p["d_skip"], l, nseq, seg)
        x = _merge(y_pad, ga, gb, x, p["w_glu"], p["w_o"], p["norm_post_mix"], l, seg)
        x = _ffn(x, p["norm_pre_ff"], p["w_ff1"], p["w_ff2"], p["norm_post_ff"], l, min(FF_ROWS, seqlen))
    return x.reshape(nseq, seqlen, d)


def kernel(x_prompt, x_sample, norm_pre_mix, w_in, norm_v, w_s, b_s, w_out_a, lam_re, lam_im,
           log_dt, b_re, b_im, c_re, c_im, d_skip, w_glu, w_o, norm_post_mix, norm_pre_ff,
           w_ff1, w_ff2, norm_post_ff):
    depth = w_in.shape[0]
    bf = lambda w: w.astype(_BF16)
    row = lambda v: v.reshape(depth, 1, D_MODEL)
    p = {
        "norm_pre_mix": row(norm_pre_mix), "w_in": bf(w_in), "norm_v": row(norm_v), "w_s": bf(w_s),
        "bias": jnp.repeat(jnp.swapaxes(b_s, 1, 2), LANES, axis=2),
        "w_out_a": bf(w_out_a), "d_skip": d_skip.reshape(depth, NBLK, 1, LANES),
        "w_glu": bf(w_glu), "w_o": bf(w_o), "norm_post_mix": row(norm_post_mix),
        "norm_pre_ff": row(norm_pre_ff), "w_ff1": bf(w_ff1), "w_ff2": bf(w_ff2),
        "norm_post_ff": row(norm_post_ff),
    }
    outs = []
    mats = {}
    for x in (x_prompt, x_sample):
        nk = x.shape[1] // (NSEG * CHUNK_L)
        if nk not in mats:
            mats[nk] = _s5_matrices(lam_re, lam_im, log_dt, b_re, b_im, c_re, c_im, nk)
        outs.append(_trunk(x, p, mats[nk]))
    return tuple(outs)
```

```python
import functools

import jax
import jax.numpy as jnp
from jax import lax
from jax.experimental import pallas as pl
from jax.experimental.pallas import tpu as pltpu

D_MODEL = 1024
GMLP_CHUNK = 128
A_GROUPS = 8
GROUPS = 64
GROUP_DIM = 16
STATE = 64
D_FF = 4 * D_MODEL
EPS = 1e-6

LANES = 128
SUBLANES = 8
CHUNK_L = 16
NSEG = SUBLANES
NBLK = D_MODEL // LANES
PAIR_LANES = 2 * GROUP_DIM
PAIRS_PER_BLK = LANES // PAIR_LANES
NPAIR = GROUPS // 2
PAIR_K = CHUNK_L * PAIR_LANES
PAIR_STATE = 2 * STATE
BLK_STATE = PAIRS_PER_BLK * PAIR_STATE
SEG_PAD = SUBLANES
ROW_BLK = 2 * SUBLANES
KPB = ROW_BLK // SUBLANES
SCAN_UNROLL = True
TOK_ROWS = 512
ROW_PARTS = 2
FF_ROWS = 1024
VMEM_LIMIT = 56 * 1024 * 1024

_F32 = jnp.float32
_BF16 = jnp.bfloat16


def _dot(a, b):
    return jnp.dot(a, b, preferred_element_type=_F32)


def _rms(x, g):
    return x * lax.rsqrt(jnp.mean(x * x, axis=-1, keepdims=True) + EPS) * g


def _sigmoid(x):
    return 1.0 / (1.0 + jnp.exp(-x))


def _gelu(x):
    return jax.nn.gelu(x, approximate=True)


def _layer_spec(shape, layer):
    nd = len(shape)
    return pl.BlockSpec((1,) + tuple(shape[1:]), lambda *_: (layer,) + (0,) * (nd - 1),
                        pipeline_mode=pl.Buffered(1))


def _params(n_axes):
    return pltpu.CompilerParams(
        dimension_semantics=("arbitrary",) * n_axes, vmem_limit_bytes=VMEM_LIMIT)


def _cmul(ar, ai, br, bi):
    return ar * br - ai * bi, ar * bi + ai * br


def _wprep_kernel(lre_ref, lim_ref, ldt_ref, bre_ref, bim_ref, cre_ref, cim_ref,
                  t_ref, wsi_ref, wso_ref, pw_ref,
                  xr_ref, xi_ref, yr_ref, yni_ref, eo_ref, *, nk):
    L = CHUNK_L
    lane = lax.broadcasted_iota(jnp.int32, (GROUP_DIM, LANES), 1)
    lane1 = lax.broadcasted_iota(jnp.int32, (1, LANES), 1)
    for d in range(2):
        a_pair = []
        for g2 in range(2):
            lre, lim = lre_ref[0, d, g2], lim_ref[0, d, g2]
            dt = jnp.exp(ldt_ref[0, d, g2])
            mag = jnp.exp(lre * dt)
            ar, ai = mag * jnp.cos(lim * dt), mag * jnp.sin(lim * dt)
            a_pair.append((ar, ai))
            den = lre * lre + lim * lim
            qr = ((ar - 1.0) * lre + ai * lim) / den
            qi = (ai * lre - (ar - 1.0) * lim) / den
            n2 = ar * ar + ai * ai
            ir, ii = ar / n2, -ai / n2
            keep = (lane >= g2 * STATE) & (lane < (g2 + 1) * STATE)
            pr, pi = _cmul(qr, qi, jnp.where(keep, bre_ref[0, d, g2], 0.0),
                           jnp.where(keep, bim_ref[0, d, g2], 0.0))
            nr, ni = pr, pi
            fr = jnp.where(keep, cre_ref[0, d, g2], 0.0)
            fi = jnp.where(keep, cim_ref[0, d, g2], 0.0)
            gr, gi = fr, fi

            def rows(k):
                return slice(k * PAIR_LANES + g2 * GROUP_DIM, k * PAIR_LANES + (g2 + 1) * GROUP_DIM)

            for k in range(L + 1):
                if k < L:
                    if d == 0:
                        xr_ref[0, rows(k), :] = nr
                        xi_ref[0, rows(k), :] = ni
                        yr_ref[0, rows(k), :] = fr
                        yni_ref[0, rows(k), :] = -fi
                        wsi_ref[0, 0, rows(L - 1 - k), 0:LANES] = pr.astype(_BF16)
                        wsi_ref[0, 0, rows(L - 1 - k), LANES:2 * LANES] = pi.astype(_BF16)
                    else:
                        xr_ref[1, rows(k), :] = pr
                        xi_ref[1, rows(k), :] = pi
                        yr_ref[1, rows(k), :] = gr
                        yni_ref[1, rows(k), :] = -gi
                        wsi_ref[0, 0, rows(k), 2 * LANES:3 * LANES] = pr.astype(_BF16)
                        wsi_ref[0, 0, rows(k), 3 * LANES:4 * LANES] = pi.astype(_BF16)
                if k >= 1:
                    t = k - 1 if d == 0 else L - k
                    eo_ref[2 * d, rows(t), :] = fr
                    eo_ref[2 * d + 1, rows(t), :] = -fi
                pr, pi = _cmul(ar, ai, pr, pi)
                nr, ni = _cmul(ir, ii, nr, ni)
                fr, fi = _cmul(ar, ai, fr, fi)
                gr, gi = _cmul(ir, ii, gr, gi)

        a_r = jnp.where(lane1 < STATE, a_pair[0][0], a_pair[1][0])
        a_i = jnp.where(lane1 < STATE, a_pair[0][1], a_pair[1][1])
        alr, ali = a_r, a_i
        for _ in range(L - 1):
            alr, ali = _cmul(alr, ali, a_r, a_i)
        one, zero = jnp.ones((1, LANES), _F32), jnp.zeros((1, LANES), _F32)
        pw_ref[0, 0, d, 0, 0:1, :] = one
        pw_ref[0, 0, d, 1, 0:1, :] = zero

        def power(k, carry):
            wr, wi = _cmul(carry[0], carry[1], alr, ali)
            pw_ref[0, 0, d, 0, pl.ds(k + 1, 1), :] = wr
            pw_ref[0, 0, d, 1, pl.ds(k + 1, 1), :] = wi
            return wr, wi

        lax.fori_loop(0, nk, power, (one, zero))

    def split(v):
        hi = v.astype(_BF16)
        return hi, (v - hi.astype(_F32)).astype(_BF16)

    def lag_matrix(d):
        xh, xl = split(jnp.concatenate([xr_ref[d], xi_ref[d]], axis=1))
        yh, yl = split(jnp.concatenate([yr_ref[d], yni_ref[d]], axis=1))
        return lax.dot_general(jnp.concatenate([xh, xh, xl], axis=1), jnp.concatenate([yh, yl, yh], axis=1),
                               (((1,), (1,)), ((), ())), preferred_element_type=_F32)

    shift = PAIR_LANES.bit_length() - 1
    tau = lax.broadcasted_iota(jnp.int32, (PAIR_K, PAIR_K), 0) >> shift
    t = lax.broadcasted_iota(jnp.int32, (PAIR_K, PAIR_K), 1) >> shift
    t_ref[0, 0] = (jnp.where(t >= tau, lag_matrix(0), 0.0)
                   + jnp.where(tau >= t, lag_matrix(1), 0.0)).astype(_BF16)
    for j in range(4):
        wso_ref[0, 0, j * LANES:(j + 1) * LANES, :] = eo_ref[j].T.astype(_BF16)


def _s5_matrices(lam_re, lam_im, log_dt, b_re, b_im, c_re, c_im, nk):
    depth = lam_re.shape[0]
    dup = lambda v: jnp.concatenate([v, v], axis=-1)
    row = lambda v: dup(v)[:, :, :, None, :]
    ldt = jnp.broadcast_to(log_dt[..., None, None], (depth, 2, GROUPS, 1, LANES))
    bt = lambda v: dup(jnp.swapaxes(v, -1, -2))
    vec = pl.BlockSpec((1, 2, 2, 1, LANES), lambda l, q: (l, 0, q, 0, 0))
    mat = pl.BlockSpec((1, 2, 2, GROUP_DIM, LANES), lambda l, q: (l, 0, q, 0, 0))
    sq = pl.BlockSpec((1, 1, PAIR_K, PAIR_K), lambda l, q: (l, q, 0, 0))
    sq_shape = jax.ShapeDtypeStruct((depth, NPAIR, PAIR_K, PAIR_K), _BF16)
    return pl.pallas_call(
        functools.partial(_wprep_kernel, nk=nk),
        grid=(depth, NPAIR),
        in_specs=[vec, vec, vec, mat, mat, mat, mat],
        out_specs=[sq, sq, sq,
                   pl.BlockSpec((1, 1, 2, 2, nk + 1, LANES), lambda l, q: (l, q, 0, 0, 0, 0))],
        out_shape=[sq_shape, sq_shape, sq_shape,
                   jax.ShapeDtypeStruct((depth, NPAIR, 2, 2, nk + 1, LANES), _F32)],
        scratch_shapes=[pltpu.VMEM((2, PAIR_K, LANES), _F32)] * 4
                       + [pltpu.VMEM((4, PAIR_K, LANES), _F32)],
        compiler_params=_params(2),
        name="s5_weight_prep",
    )(row(lam_re), row(lam_im), ldt, bt(b_re), bt(b_im), dup(c_re), dup(c_im))


def _inproj_kernel(x_ref, npm_ref, win_ref, nv_ref, ws_ref, bs_ref, woa_ref,
                   ga_ref, u_ref, gb_ref, mix_ref, *, rows, tiles_per_seg, seg):
    d = D_MODEL
    prow = rows // ROW_PARTS
    nchunk = prow // GMLP_CHUNK
    parts = [slice(q * prow, (q + 1) * prow) for q in range(ROW_PARTS)]
    h = [_rms(x_ref[p, :], npm_ref[0]).astype(_BF16) for p in parts]
    vn = [_rms(_gelu(_dot(hq, win_ref[0, :, d:2 * d])), nv_ref[0]).astype(_BF16) for hq in h]
    r0 = pl.multiple_of((pl.program_id(0) % tiles_per_seg) * rows, rows)
    for p, hq in zip(parts, h):
        ub = _dot(hq, win_ref[0, :, 2 * d:3 * d])
        for b in range(NBLK):
            u_ref[b, pl.ds(r0 + p.start, prow), :] = ub[:, b * LANES:(b + 1) * LANES]
        gb_ref[p, :] = _sigmoid(_dot(hq, win_ref[0, :, 4 * d:5 * d])).astype(_BF16)
    u_ref[:, seg:seg + SEG_PAD, :] = jnp.zeros((NBLK, SEG_PAD, LANES), _F32)
    ua = [_gelu(_dot(hq, win_ref[0, :, 0:d])) for hq in h]
    for p, vq in zip(parts, vn):
        for g in range(A_GROUPS):
            cs = slice(g * LANES, (g + 1) * LANES)
            v_all = jnp.concatenate([vq[n * GMLP_CHUNK:(n + 1) * GMLP_CHUNK, cs] for n in range(nchunk)], axis=1)
            mixed = _dot(ws_ref[0, g], v_all)
            for n in range(nchunk):
                mix_ref[p.start + n * GMLP_CHUNK:p.start + (n + 1) * GMLP_CHUNK, cs] = (
                    mixed[:, n * LANES:(n + 1) * LANES] + bs_ref[0, :, cs])
    for p, hq, uq in zip(parts, h, ua):
        a = _dot((uq * mix_ref[p, :]).astype(_BF16), woa_ref[0])
        ga_ref[p, :] = (_sigmoid(_dot(hq, win_ref[0, :, 3 * d:4 * d])) * a).astype(_BF16)


def _inproj(x, npm, w_in, nv, w_s, bias, w_out_a, layer, seg):
    ntok = x.shape[0]
    rows = min(TOK_ROWS, seg)
    tps = seg // rows
    pitch = seg + SEG_PAD
    tok = pl.BlockSpec((rows, D_MODEL), lambda i: (i, 0))
    return pl.pallas_call(
        functools.partial(_inproj_kernel, rows=rows, tiles_per_seg=tps, seg=seg),
        grid=(ntok // rows,),
        in_specs=[tok] + [_layer_spec(w.shape, layer) for w in (npm, w_in, nv, w_s, bias, w_out_a)],
        out_specs=[tok, pl.BlockSpec((NBLK, pitch, LANES), lambda i: (0, i // tps, 0)), tok],
        out_shape=[jax.ShapeDtypeStruct((ntok, D_MODEL), _BF16),
                   jax.ShapeDtypeStruct((NBLK, ntok // seg * pitch, LANES), _F32),
                   jax.ShapeDtypeStruct((ntok, D_MODEL), _BF16)],
        scratch_shapes=[pltpu.VMEM((rows, D_MODEL), _F32)],
        compiler_params=_params(1),
        name="inproj_gmlp",
    )(x, npm, w_in, nv, w_s, bias, w_out_a)


def _s5_kernel(u_ref, t_ref, wi_ref, wo_ref, pw_ref, skip_ref, perm_ref, y_ref,
               raw_ref, lhs_ref, z_ref, s_ref, yp_ref, yf_ref, *, nk, pitch):
    ns = BLK_STATE
    tgroups = CHUNK_L // PAIRS_PER_BLK
    tg_cols = PAIRS_PER_BLK * LANES

    def seg_rows(k, tau):
        return pl.ds(k * CHUNK_L + tau, NSEG, stride=pitch)

    def power_row(d, c, row):
        return jnp.concatenate([pw_ref[0, j, d, c, row, :] for j in range(PAIRS_PER_BLK)], axis=1)

    def gather(tg):
        def body(i, _):
            r = pl.ds(pl.multiple_of(i * ROW_BLK, ROW_BLK), ROW_BLK)
            for tau in range(tg * PAIRS_PER_BLK, (tg + 1) * PAIRS_PER_BLK):
                v = jnp.concatenate([u_ref[seg_rows(i * KPB + kk, tau), :] for kk in range(KPB)], axis=0)
                raw_ref[r, tau * LANES:(tau + 1) * LANES] = v.astype(_BF16)
            return 0

        lax.fori_loop(0, nk // KPB, body, 0, unroll=True)

    for tg in range(tgroups):
        gather(tg)
        pb = _dot(raw_ref[:, tg * tg_cols:(tg + 1) * tg_cols], perm_ref[...])
        for j in range(PAIRS_PER_BLK):
            lhs_ref[j, :, tg * LANES:(tg + 1) * LANES] = pb[:, j * LANES:(j + 1) * LANES].astype(_BF16)

    seg_id = lax.broadcasted_iota(jnp.int32, (NSEG, ns), 0)
    for d in range(2):
        for j in range(PAIRS_PER_BLK):
            zz = _dot(lhs_ref[j], wi_ref[0, j, :, d * 2 * LANES:(d + 1) * 2 * LANES])
            z_ref[d, :, j * LANES:(j + 1) * LANES] = zz[:, 0:LANES]
            z_ref[d, :, ns + j * LANES:ns + (j + 1) * LANES] = zz[:, LANES:2 * LANES]
    for j in range(PAIRS_PER_BLK):
        yp_ref[j] = _dot(lhs_ref[j], t_ref[0, j])

    for d in range(2):
        backward = d == 1
        al_r = jnp.broadcast_to(power_row(d, 0, slice(1, 2)), (NSEG, ns))
        al_i = jnp.broadcast_to(power_row(d, 1, slice(1, 2)), (NSEG, ns))

        def local_scan(j, carry):
            k = nk - 1 - j if backward else j
            r = pl.ds(pl.multiple_of(k * SUBLANES, SUBLANES), SUBLANES)
            sr, si = carry
            zr, zi = z_ref[d, r, 0:ns], z_ref[d, r, ns:2 * ns]
            z_ref[d, r, 0:ns] = sr
            z_ref[d, r, ns:2 * ns] = si
            pr, pi = _cmul(al_r, al_i, sr, si)
            return pr + zr, pi + zi

        zero = jnp.zeros((NSEG, ns), _F32)
        tot_r, tot_i = lax.fori_loop(0, nk, local_scan, (zero, zero), unroll=SCAN_UNROLL)

        as_r = jnp.broadcast_to(power_row(d, 0, slice(nk, nk + 1)), (NSEG, ns))
        as_i = jnp.broadcast_to(power_row(d, 1, slice(nk, nk + 1)), (NSEG, ns))
        edge = NSEG - 1 if backward else 0
        shift = NSEG - 1 if backward else 1
        cr, ci = zero, zero
        for _ in range(NSEG - 1):
            pr, pi = _cmul(as_r, as_i, cr, ci)
            cr = jnp.where(seg_id == edge, 0.0, pltpu.roll(pr + tot_r, shift, 0))
            ci = jnp.where(seg_id == edge, 0.0, pltpu.roll(pi + tot_i, shift, 0))

        def add_carry(i, _):
            r = pl.ds(pl.multiple_of(i * ROW_BLK, ROW_BLK), ROW_BLK)
            sr, si = [], []
            for kk in range(KPB):
                k = i * KPB + kk
                e = nk - 1 - k if backward else k
                pr, pi = _cmul(power_row(d, 0, pl.ds(e, 1)), power_row(d, 1, pl.ds(e, 1)), cr, ci)
                rk = pl.ds(pl.multiple_of(k * SUBLANES, SUBLANES), SUBLANES)
                sr.append(z_ref[d, rk, 0:ns] + pr)
                si.append(z_ref[d, rk, ns:2 * ns] + pi)
            s_ref[r, d * 2 * ns:d * 2 * ns + ns] = jnp.concatenate(sr, axis=0).astype(_BF16)
            s_ref[r, d * 2 * ns + ns:(d + 1) * 2 * ns] = jnp.concatenate(si, axis=0).astype(_BF16)
            return 0

        lax.fori_loop(0, nk // KPB, add_carry, 0, unroll=SCAN_UNROLL)

    for j in range(PAIRS_PER_BLK):
        sp = jnp.concatenate([s_ref[:, q * ns + j * LANES:q * ns + (j + 1) * LANES] for q in range(4)],
                             axis=1)
        yp_ref[j] = yp_ref[j] + _dot(sp, wo_ref[0, j])

    skip = skip_ref[0, 0]

    def scatter(tg):
        def body(k, _):
            r = pl.ds(pl.multiple_of(k * SUBLANES, SUBLANES), SUBLANES)
            for t in range(tg * PAIRS_PER_BLK, (tg + 1) * PAIRS_PER_BLK):
                rows = seg_rows(k, t)
                y_ref[rows, :] = yf_ref[r, t * LANES:(t + 1) * LANES] + skip * u_ref[rows, :]
            return 0

        lax.fori_loop(0, nk, body, 0, unroll=True)

    for tg in range(tgroups):
        yc = jnp.concatenate([yp_ref[j, :, tg * LANES:(tg + 1) * LANES] for j in range(PAIRS_PER_BLK)],
                             axis=1)
        yf_ref[:, tg * tg_cols:(tg + 1) * tg_cols] = _dot(yc.astype(_BF16), perm_ref[...])
        scatter(tg)
    seg = nk * CHUNK_L
    for s in range(NSEG):
        y_ref[s * pitch + seg:(s + 1) * pitch, :] = jnp.zeros((SEG_PAD, LANES), _F32)


def _pair_permutation():
    n = PAIRS_PER_BLK
    src = jnp.arange(n * LANES)
    tl, j, c = src // LANES, (src // PAIR_LANES) % n, src % PAIR_LANES
    dst = j * LANES + tl * PAIR_LANES + c
    return (dst[:, None] == jnp.arange(n * LANES)[None, :]).astype(_BF16)


def _s5_core(u_pad, t_mat, w_si, w_so, pw, skip, layer, nseq, seg):
    nk = seg // CHUNK_L
    pitch = seg + SEG_PAD
    m = nk * NSEG
    perm = _pair_permutation()
    seq_blk = pl.BlockSpec((None, NSEG * pitch, LANES), lambda b, s: (b, s, 0))
    sq = pl.BlockSpec((1, PAIRS_PER_BLK, PAIR_K, PAIR_K), lambda b, s: (layer, b, 0, 0))
    return pl.pallas_call(
        functools.partial(_s5_kernel, nk=nk, pitch=pitch),
        grid=(NBLK, nseq),
        in_specs=[seq_blk, sq, sq, sq,
                  pl.BlockSpec((1, PAIRS_PER_BLK, 2, 2, nk + 1, LANES), lambda b, s: (layer, b, 0, 0, 0, 0)),
                  pl.BlockSpec((1, 1, 1, LANES), lambda b, s: (layer, b, 0, 0)),
                  pl.BlockSpec(perm.shape, lambda b, s: (0, 0))],
        out_specs=seq_blk,
        out_shape=jax.ShapeDtypeStruct(u_pad.shape, _F32),
        scratch_shapes=[pltpu.VMEM((m, CHUNK_L * LANES), _BF16),
                        pltpu.VMEM((PAIRS_PER_BLK, m, PAIR_K), _BF16),
                        pltpu.VMEM((2, m, 2 * BLK_STATE), _F32),
                        pltpu.VMEM((m, 4 * BLK_STATE), _BF16),
                        pltpu.VMEM((PAIRS_PER_BLK, m, PAIR_K), _F32),
                        pltpu.VMEM((m, CHUNK_L * LANES), _F32)],
        compiler_params=_params(2),
        name="s5_core",
    )(u_pad, t_mat, w_si, w_so, pw, skip, perm)


def _merge_kernel(y_ref, ga_ref, gb_ref, x_ref, wglu_ref, wo_ref, npost_ref, o_ref, *, rows, tiles_per_seg):
    d = D_MODEL
    r0 = pl.multiple_of((pl.program_id(0) % tiles_per_seg) * rows, rows)
    parts = [slice(q * (rows // ROW_PARTS), (q + 1) * (rows // ROW_PARTS)) for q in range(ROW_PARTS)]
    def y_rows(p):
        return jnp.concatenate([y_ref[b, pl.ds(r0 + p.start, p.stop - p.start), :] for b in range(NBLK)], axis=1)

    z = [_gelu(y_rows(p)).astype(_BF16) for p in parts]
    g1 = [_dot(zq, wglu_ref[0, :, 0:d]) for zq in z]
    g2 = [_dot(zq, wglu_ref[0, :, d:2 * d]) for zq in z]
    m = [(ga_ref[p, :].astype(_F32) + gb_ref[p, :].astype(_F32) * (a * _sigmoid(b))).astype(_BF16)
         for p, a, b in zip(parts, g1, g2)]
    for p, mq in zip(parts, m):
        o_ref[p, :] = x_ref[p, :] + _rms(_dot(mq, wo_ref[0]), npost_ref[0])


def _merge(y_pad, ga, gb, x, w_glu, w_o, npost, layer, seg):
    ntok = x.shape[0]
    rows = min(TOK_ROWS, seg)
    tps = seg // rows
    pitch = seg + SEG_PAD
    tok = pl.BlockSpec((rows, D_MODEL), lambda i: (i, 0))
    return pl.pallas_call(
        functools.partial(_merge_kernel, rows=rows, tiles_per_seg=tps),
        grid=(ntok // rows,),
        in_specs=[pl.BlockSpec((NBLK, pitch, LANES), lambda i: (0, i // tps, 0)), tok, tok, tok]
                 + [_layer_spec(w.shape, layer) for w in (w_glu, w_o, npost)],
        out_specs=tok,
        out_shape=jax.ShapeDtypeStruct((ntok, D_MODEL), _F32),
        compiler_params=_params(1),
        name="glu_merge",
    )(y_pad, ga, gb, x, w_glu, w_o, npost)


def _ffn_kernel(x_ref, npre_ref, w1_ref, w2_ref, npost_ref, o_ref):
    x = x_ref[...]
    h = _rms(x, npre_ref[0]).astype(_BF16)
    f = jnp.zeros(x.shape, _F32)
    for j in range(D_FF // D_MODEL):
        cs = slice(j * D_MODEL, (j + 1) * D_MODEL)
        r = jnp.maximum(_dot(h, w1_ref[0, :, cs]), 0.0)
        f = f + _dot((r * r).astype(_BF16), w2_ref[0, cs, :])
    o_ref[...] = x + _rms(f, npost_ref[0])


def _ffn(x, npre, w1, w2, npost, layer, rows):
    ntok = x.shape[0]
    tok = pl.BlockSpec((rows, D_MODEL), lambda i: (i, 0))
    return pl.pallas_call(
        _ffn_kernel,
        grid=(ntok // rows,),
        in_specs=[tok] + [_layer_spec(w.shape, layer) for w in (npre, w1, w2, npost)],
        out_specs=tok,
        out_shape=jax.ShapeDtypeStruct((ntok, D_MODEL), _F32),
        compiler_params=_params(1),
        name="ffn",
    )(x, npre, w1, w2, npost)


def _trunk(x, p, s5m):
    nseq, seqlen, d = x.shape
    seg = seqlen // NSEG
    assert d == D_MODEL and seg % GMLP_CHUNK == 0 and seqlen % TOK_ROWS == 0
    x = x.reshape(nseq * seqlen, d)
    t_mat, w_si, w_so, pw = s5m
    for l in range(p["w_in"].shape[0]):
        ga, u_pad, gb = _inproj(x, p["norm_pre_mix"], p["w_in"], p["norm_v"], p["w_s"], p["bias"],
                                p["w_out_a"], l, seg)
        y_pad = _s5_core(u_pad, t_mat, w_si, w_so, pw, p["d_skip"], l, nseq, seg)
        x = _merge(y_pad, ga, gb, x, p["w_glu"], p["w_o"], p["norm_post_mix"], l, seg)
        x = _ffn(x, p["norm_pre_ff"], p["w_ff1"], p["w_ff2"], p["norm_post_ff"], l, min(FF_ROWS, seqlen))
    return x.reshape(nseq, seqlen, d)


def kernel(x_prompt, x_sample, norm_pre_mix, w_in, norm_v, w_s, b_s, w_out_a, lam_re, lam_im,
           log_dt, b_re, b_im, c_re, c_im, d_skip, w_glu, w_o, norm_post_mix, norm_pre_ff,
           w_ff1, w_ff2, norm_post_ff):
    depth = w_in.shape[0]
    bf = lambda w: w.astype(_BF16)
    row = lambda v: v.reshape(depth, 1, D_MODEL)
    p = {
        "norm_pre_mix": row(norm_pre_mix), "w_in": bf(w_in), "norm_v": row(norm_v), "w_s": bf(w_s),
        "bias": jnp.repeat(jnp.swapaxes(b_s, 1, 2), LANES, axis=2),
        "w_out_a": bf(w_out_a), "d_skip": d_skip.reshape(depth, NBLK, 1, LANES),
        "w_glu": bf(w_glu), "w_o": bf(w_o), "norm_post_mix": row(norm_post_mix),
        "norm_pre_ff": row(norm_pre_ff), "w_ff1": bf(w_ff1), "w_ff2": bf(w_ff2),
        "norm_post_ff": row(norm_post_ff),
    }
    outs = []
    mats = {}
    for x in (x_prompt, x_sample):
        nk = x.shape[1] // (NSEG * CHUNK_L)
        if nk not in mats:
            mats[nk] = _s5_matrices(lam_re, lam_im, log_dt, b_re, b_im, c_re, c_im, nk)
        outs.append(_trunk(x, p, mats[nk]))
    return tuple(outs)
```

```python
import functools

import jax
import jax.numpy as jnp
from jax import lax
from jax.experimental import pallas as pl
from jax.experimental.pallas import tpu as pltpu

D_MODEL = 1024
GMLP_CHUNK = 128
A_GROUPS = 8
GROUPS = 64
GROUP_DIM = 16
STATE = 64
D_FF = 4 * D_MODEL
EPS = 1e-6

LANES = 128
SUBLANES = 8
CHUNK_L = 16
NSEG = SUBLANES
NBLK = D_MODEL // LANES
PAIR_LANES = 2 * GROUP_DIM
PAIRS_PER_BLK = LANES // PAIR_LANES
NPAIR = GROUPS // 2
PAIR_K = CHUNK_L * PAIR_LANES
PAIR_STATE = 2 * STATE
BLK_STATE = PAIRS_PER_BLK * PAIR_STATE
SEG_PAD = SUBLANES
ROW_BLK = 2 * SUBLANES
KPB = ROW_BLK // SUBLANES
SCAN_UNROLL = True
TOK_ROWS = 512
ROW_PARTS = 2
VMEM_LIMIT = 56 * 1024 * 1024
FUSED_VMEM_LIMIT = 60 * 1024 * 1024

_F32 = jnp.float32
_BF16 = jnp.bfloat16


def _dot(a, b):
    return jnp.dot(a, b, preferred_element_type=_F32)


def _rms(x, g):
    return x * lax.rsqrt(jnp.mean(x * x, axis=-1, keepdims=True) + EPS) * g


def _sigmoid(x):
    return 1.0 / (1.0 + jnp.exp(-x))


def _gelu(x):
    return jax.nn.gelu(x, approximate=True)


def _layer_spec(shape, layer):
    nd = len(shape)
    return pl.BlockSpec((1,) + tuple(shape[1:]), lambda *_: (layer,) + (0,) * (nd - 1),
                        pipeline_mode=pl.Buffered(1))


def _params(n_axes):
    return pltpu.CompilerParams(
        dimension_semantics=("arbitrary",) * n_axes, vmem_limit_bytes=VMEM_LIMIT)


def _cmul(ar, ai, br, bi):
    return ar * br - ai * bi, ar * bi + ai * br


def _wprep_kernel(lre_ref, lim_ref, ldt_ref, bre_ref, bim_ref, cre_ref, cim_ref,
                  t_ref, wsi_ref, wso_ref, pw_ref,
                  xr_ref, xi_ref, yr_ref, yni_ref, eo_ref, *, nk):
    L = CHUNK_L
    lane = lax.broadcasted_iota(jnp.int32, (GROUP_DIM, LANES), 1)
    lane1 = lax.broadcasted_iota(jnp.int32, (1, LANES), 1)
    for d in range(2):
        a_pair = []
        for g2 in range(2):
            lre, lim = lre_ref[0, d, g2], lim_ref[0, d, g2]
            dt = jnp.exp(ldt_ref[0, d, g2])
            mag = jnp.exp(lre * dt)
            ar, ai = mag * jnp.cos(lim * dt), mag * jnp.sin(lim * dt)
            a_pair.append((ar, ai))
            den = lre * lre + lim * lim
            qr = ((ar - 1.0) * lre + ai * lim) / den
            qi = (ai * lre - (ar - 1.0) * lim) / den
            n2 = ar * ar + ai * ai
            ir, ii = ar / n2, -ai / n2
            keep = (lane >= g2 * STATE) & (lane < (g2 + 1) * STATE)
            pr, pi = _cmul(qr, qi, jnp.where(keep, bre_ref[0, d, g2], 0.0),
                           jnp.where(keep, bim_ref[0, d, g2], 0.0))
            nr, ni = pr, pi
            fr = jnp.where(keep, cre_ref[0, d, g2], 0.0)
            fi = jnp.where(keep, cim_ref[0, d, g2], 0.0)
            gr, gi = fr, fi

            def rows(k):
                return slice(k * PAIR_LANES + g2 * GROUP_DIM, k * PAIR_LANES + (g2 + 1) * GROUP_DIM)

            for k in range(L + 1):
                if k < L:
                    if d == 0:
                        xr_ref[0, rows(k), :] = nr
                        xi_ref[0, rows(k), :] = ni
                        yr_ref[0, rows(k), :] = fr
                        yni_ref[0, rows(k), :] = -fi
                        wsi_ref[0, 0, rows(L - 1 - k), 0:LANES] = pr.astype(_BF16)
                        wsi_ref[0, 0, rows(L - 1 - k), LANES:2 * LANES] = pi.astype(_BF16)
                    else:
                        xr_ref[1, rows(k), :] = pr
                        xi_ref[1, rows(k), :] = pi
                        yr_ref[1, rows(k), :] = gr
                        yni_ref[1, rows(k), :] = -gi
                        wsi_ref[0, 0, rows(k), 2 * LANES:3 * LANES] = pr.astype(_BF16)
                        wsi_ref[0, 0, rows(k), 3 * LANES:4 * LANES] = pi.astype(_BF16)
                if k >= 1:
                    t = k - 1 if d == 0 else L - k
                    eo_ref[2 * d, rows(t), :] = fr
                    eo_ref[2 * d + 1, rows(t), :] = -fi
                pr, pi = _cmul(ar, ai, pr, pi)
                nr, ni = _cmul(ir, ii, nr, ni)
                fr, fi = _cmul(ar, ai, fr, fi)
                gr, gi = _cmul(ir, ii, gr, gi)

        a_r = jnp.where(lane1 < STATE, a_pair[0][0], a_pair[1][0])
        a_i = jnp.where(lane1 < STATE, a_pair[0][1], a_pair[1][1])
        alr, ali = a_r, a_i
        for _ in range(L - 1):
            alr, ali = _cmul(alr, ali, a_r, a_i)
        one, zero = jnp.ones((1, LANES), _F32), jnp.zeros((1, LANES), _F32)
        pw_ref[0, 0, d, 0, 0:1, :] = one
        pw_ref[0, 0, d, 1, 0:1, :] = zero

        def power(k, carry):
            wr, wi = _cmul(carry[0], carry[1], alr, ali)
            pw_ref[0, 0, d, 0, pl.ds(k + 1, 1), :] = wr
            pw_ref[0, 0, d, 1, pl.ds(k + 1, 1), :] = wi
            return wr, wi

        lax.fori_loop(0, nk, power, (one, zero))

    def split(v):
        hi = v.astype(_BF16)
        return hi, (v - hi.astype(_F32)).astype(_BF16)

    def lag_matrix(d):
        xh, xl = split(jnp.concatenate([xr_ref[d], xi_ref[d]], axis=1))
        yh, yl = split(jnp.concatenate([yr_ref[d], yni_ref[d]], axis=1))
        return lax.dot_general(jnp.concatenate([xh, xh, xl], axis=1), jnp.concatenate([yh, yl, yh], axis=1),
                               (((1,), (1,)), ((), ())), preferred_element_type=_F32)

    shift = PAIR_LANES.bit_length() - 1
    tau = lax.broadcasted_iota(jnp.int32, (PAIR_K, PAIR_K), 0) >> shift
    t = lax.broadcasted_iota(jnp.int32, (PAIR_K, PAIR_K), 1) >> shift
    t_ref[0, 0] = (jnp.where(t >= tau, lag_matrix(0), 0.0)
                   + jnp.where(tau >= t, lag_matrix(1), 0.0)).astype(_BF16)
    for j in range(4):
        wso_ref[0, 0, j * LANES:(j + 1) * LANES, :] = eo_ref[j].T.astype(_BF16)


def _s5_matrices(lam_re, lam_im, log_dt, b_re, b_im, c_re, c_im, nk):
    depth = lam_re.shape[0]
    dup = lambda v: jnp.concatenate([v, v], axis=-1)
    row = lambda v: dup(v)[:, :, :, None, :]
    ldt = jnp.broadcast_to(log_dt[..., None, None], (depth, 2, GROUPS, 1, LANES))
    bt = lambda v: dup(jnp.swapaxes(v, -1, -2))
    vec = pl.BlockSpec((1, 2, 2, 1, LANES), lambda l, q: (l, 0, q, 0, 0))
    mat = pl.BlockSpec((1, 2, 2, GROUP_DIM, LANES), lambda l, q: (l, 0, q, 0, 0))
    sq = pl.BlockSpec((1, 1, PAIR_K, PAIR_K), lambda l, q: (l, q, 0, 0))
    sq_shape = jax.ShapeDtypeStruct((depth, NPAIR, PAIR_K, PAIR_K), _BF16)
    return pl.pallas_call(
        functools.partial(_wprep_kernel, nk=nk),
        grid=(depth, NPAIR),
        in_specs=[vec, vec, vec, mat, mat, mat, mat],
        out_specs=[sq, sq, sq,
                   pl.BlockSpec((1, 1, 2, 2, nk + 1, LANES), lambda l, q: (l, q, 0, 0, 0, 0))],
        out_shape=[sq_shape, sq_shape, sq_shape,
                   jax.ShapeDtypeStruct((depth, NPAIR, 2, 2, nk + 1, LANES), _F32)],
        scratch_shapes=[pltpu.VMEM((2, PAIR_K, LANES), _F32)] * 4
                       + [pltpu.VMEM((4, PAIR_K, LANES), _F32)],
        compiler_params=_params(2),
        name="s5_weight_prep",
    )(row(lam_re), row(lam_im), ldt, bt(b_re), bt(b_im), dup(c_re), dup(c_im))


def _inproj_kernel(x_ref, npm_ref, win_ref, nv_ref, ws_ref, bs_ref, woa_ref,
                   ga_ref, u_ref, gb_ref, mix_ref, *, rows, tiles_per_seg, seg):
    d = D_MODEL
    prow = rows // ROW_PARTS
    nchunk = prow // GMLP_CHUNK
    parts = [slice(q * prow, (q + 1) * prow) for q in range(ROW_PARTS)]
    h = [_rms(x_ref[p, :], npm_ref[0]).astype(_BF16) for p in parts]
    vn = [_rms(_gelu(_dot(hq, win_ref[0, :, d:2 * d])), nv_ref[0]).astype(_BF16) for hq in h]
    r0 = pl.multiple_of((pl.program_id(0) % tiles_per_seg) * rows, rows)
    for p, hq in zip(parts, h):
        ub = _dot(hq, win_ref[0, :, 2 * d:3 * d])
        for b in range(NBLK):
            u_ref[b, pl.ds(r0 + p.start, prow), :] = ub[:, b * LANES:(b + 1) * LANES]
        gb_ref[p, :] = _sigmoid(_dot(hq, win_ref[0, :, 4 * d:5 * d])).astype(_BF16)
    u_ref[:, seg:seg + SEG_PAD, :] = jnp.zeros((NBLK, SEG_PAD, LANES), _F32)
    ua = [_gelu(_dot(hq, win_ref[0, :, 0:d])) for hq in h]
    for p, vq in zip(parts, vn):
        for g in range(A_GROUPS):
            cs = slice(g * LANES, (g + 1) * LANES)
            v_all = jnp.concatenate([vq[n * GMLP_CHUNK:(n + 1) * GMLP_CHUNK, cs] for n in range(nchunk)], axis=1)
            mixed = _dot(ws_ref[0, g], v_all)
            for n in range(nchunk):
                mix_ref[p.start + n * GMLP_CHUNK:p.start + (n + 1) * GMLP_CHUNK, cs] = (
                    mixed[:, n * LANES:(n + 1) * LANES] + bs_ref[0, :, cs])
    for p, hq, uq in zip(parts, h, ua):
        a = _dot((uq * mix_ref[p, :]).astype(_BF16), woa_ref[0])
        ga_ref[p, :] = (_sigmoid(_dot(hq, win_ref[0, :, 3 * d:4 * d])) * a).astype(_BF16)


def _inproj(x, npm, w_in, nv, w_s, bias, w_out_a, layer, seg):
    ntok = x.shape[0]
    rows = min(TOK_ROWS, seg)
    tps = seg // rows
    pitch = seg + SEG_PAD
    tok = pl.BlockSpec((rows, D_MODEL), lambda i: (i, 0))
    return pl.pallas_call(
        functools.partial(_inproj_kernel, rows=rows, tiles_per_seg=tps, seg=seg),
        grid=(ntok // rows,),
        in_specs=[tok] + [_layer_spec(w.shape, layer) for w in (npm, w_in, nv, w_s, bias, w_out_a)],
        out_specs=[tok, pl.BlockSpec((NBLK, pitch, LANES), lambda i: (0, i // tps, 0)), tok],
        out_shape=[jax.ShapeDtypeStruct((ntok, D_MODEL), _BF16),
                   jax.ShapeDtypeStruct((NBLK, ntok // seg * pitch, LANES), _F32),
                   jax.ShapeDtypeStruct((ntok, D_MODEL), _BF16)],
        scratch_shapes=[pltpu.VMEM((rows, D_MODEL), _F32)],
        compiler_params=_params(1),
        name="inproj_gmlp",
    )(x, npm, w_in, nv, w_s, bias, w_out_a)


def _s5_kernel(u_ref, t_ref, wi_ref, wo_ref, pw_ref, skip_ref, perm_ref, y_ref,
               raw_ref, lhs_ref, z_ref, s_ref, yp_ref, yf_ref, *, nk, pitch):
    ns = BLK_STATE
    tgroups = CHUNK_L // PAIRS_PER_BLK
    tg_cols = PAIRS_PER_BLK * LANES

    def seg_rows(k, tau):
        return pl.ds(k * CHUNK_L + tau, NSEG, stride=pitch)

    def power_row(d, c, row):
        return jnp.concatenate([pw_ref[0, j, d, c, row, :] for j in range(PAIRS_PER_BLK)], axis=1)

    def gather(tg):
        def body(i, _):
            r = pl.ds(pl.multiple_of(i * ROW_BLK, ROW_BLK), ROW_BLK)
            for tau in range(tg * PAIRS_PER_BLK, (tg + 1) * PAIRS_PER_BLK):
                v = jnp.concatenate([u_ref[seg_rows(i * KPB + kk, tau), :] for kk in range(KPB)], axis=0)
                raw_ref[r, tau * LANES:(tau + 1) * LANES] = v.astype(_BF16)
            return 0

        lax.fori_loop(0, nk // KPB, body, 0, unroll=True)

    for tg in range(tgroups):
        gather(tg)
        pb = _dot(raw_ref[:, tg * tg_cols:(tg + 1) * tg_cols], perm_ref[...])
        for j in range(PAIRS_PER_BLK):
            lhs_ref[j, :, tg * LANES:(tg + 1) * LANES] = pb[:, j * LANES:(j + 1) * LANES].astype(_BF16)

    seg_id = lax.broadcasted_iota(jnp.int32, (NSEG, ns), 0)
    for d in range(2):
        for j in range(PAIRS_PER_BLK):
            zz = _dot(lhs_ref[j], wi_ref[0, j, :, d * 2 * LANES:(d + 1) * 2 * LANES])
            z_ref[d, :, j * LANES:(j + 1) * LANES] = zz[:, 0:LANES]
            z_ref[d, :, ns + j * LANES:ns + (j + 1) * LANES] = zz[:, LANES:2 * LANES]
    for j in range(PAIRS_PER_BLK):
        yp_ref[j] = _dot(lhs_ref[j], t_ref[0, j])

    for d in range(2):
        backward = d == 1
        al_r = jnp.broadcast_to(power_row(d, 0, slice(1, 2)), (NSEG, ns))
        al_i = jnp.broadcast_to(power_row(d, 1, slice(1, 2)), (NSEG, ns))

        def local_scan(j, carry):
            k = nk - 1 - j if backward else j
            r = pl.ds(pl.multiple_of(k * SUBLANES, SUBLANES), SUBLANES)
            sr, si = carry
            zr, zi = z_ref[d, r, 0:ns], z_ref[d, r, ns:2 * ns]
            z_ref[d, r, 0:ns] = sr
            z_ref[d, r, ns:2 * ns] = si
            pr, pi = _cmul(al_r, al_i, sr, si)
            return pr + zr, pi + zi

        zero = jnp.zeros((NSEG, ns), _F32)
        tot_r, tot_i = lax.fori_loop(0, nk, local_scan, (zero, zero), unroll=SCAN_UNROLL)

        as_r = jnp.broadcast_to(power_row(d, 0, slice(nk, nk + 1)), (NSEG, ns))
        as_i = jnp.broadcast_to(power_row(d, 1, slice(nk, nk + 1)), (NSEG, ns))
        edge = NSEG - 1 if backward else 0
        shift = NSEG - 1 if backward else 1
        cr, ci = zero, zero
        for _ in range(NSEG - 1):
            pr, pi = _cmul(as_r, as_i, cr, ci)
            cr = jnp.where(seg_id == edge, 0.0, pltpu.roll(pr + tot_r, shift, 0))
            ci = jnp.where(seg_id == edge, 0.0, pltpu.roll(pi + tot_i, shift, 0))

        def add_carry(i, _):
            r = pl.ds(pl.multiple_of(i * ROW_BLK, ROW_BLK), ROW_BLK)
            sr, si = [], []
            for kk in range(KPB):
                k = i * KPB + kk
                e = nk - 1 - k if backward else k
                pr, pi = _cmul(power_row(d, 0, pl.ds(e, 1)), power_row(d, 1, pl.ds(e, 1)), cr, ci)
                rk = pl.ds(pl.multiple_of(k * SUBLANES, SUBLANES), SUBLANES)
                sr.append(z_ref[d, rk, 0:ns] + pr)
                si.append(z_ref[d, rk, ns:2 * ns] + pi)
            s_ref[r, d * 2 * ns:d * 2 * ns + ns] = jnp.concatenate(sr, axis=0).astype(_BF16)
            s_ref[r, d * 2 * ns + ns:(d + 1) * 2 * ns] = jnp.concatenate(si, axis=0).astype(_BF16)
            return 0

        lax.fori_loop(0, nk // KPB, add_carry, 0, unroll=SCAN_UNROLL)

    for j in range(PAIRS_PER_BLK):
        sp = jnp.concatenate([s_ref[:, q * ns + j * LANES:q * ns + (j + 1) * LANES] for q in range(4)],
                             axis=1)
        yp_ref[j] = yp_ref[j] + _dot(sp, wo_ref[0, j])

    skip = skip_ref[0, 0]

    def scatter(tg):
        def body(k, _):
            r = pl.ds(pl.multiple_of(k * SUBLANES, SUBLANES), SUBLANES)
            for t in range(tg * PAIRS_PER_BLK, (tg + 1) * PAIRS_PER_BLK):
                rows = seg_rows(k, t)
                y_ref[rows, :] = yf_ref[r, t * LANES:(t + 1) * LANES] + skip * u_ref[rows, :]
            return 0

        lax.fori_loop(0, nk, body, 0, unroll=True)

    for tg in range(tgroups):
        yc = jnp.concatenate([yp_ref[j, :, tg * LANES:(tg + 1) * LANES] for j in range(PAIRS_PER_BLK)],
                             axis=1)
        yf_ref[:, tg * tg_cols:(tg + 1) * tg_cols] = _dot(yc.astype(_BF16), perm_ref[...])
        scatter(tg)
    seg = nk * CHUNK_L
    for s in range(NSEG):
        y_ref[s * pitch + seg:(s + 1) * pitch, :] = jnp.zeros((SEG_PAD, LANES), _F32)


def _pair_permutation():
    n = PAIRS_PER_BLK
    src = jnp.arange(n * LANES)
    tl, j, c = src // LANES, (src // PAIR_LANES) % n, src % PAIR_LANES
    dst = j * LANES + tl * PAIR_LANES + c
    return (dst[:, None] == jnp.arange(n * LANES)[None, :]).astype(_BF16)


def _s5_core(u_pad, t_mat, w_si, w_so, pw, skip, layer, nseq, seg):
    nk = seg // CHUNK_L
    pitch = seg + SEG_PAD
    m = nk * NSEG
    perm = _pair_permutation()
    seq_blk = pl.BlockSpec((None, NSEG * pitch, LANES), lambda b, s: (b, s, 0))
    sq = pl.BlockSpec((1, PAIRS_PER_BLK, PAIR_K, PAIR_K), lambda b, s: (layer, b, 0, 0))
    return pl.pallas_call(
        functools.partial(_s5_kernel, nk=nk, pitch=pitch),
        grid=(NBLK, nseq),
        in_specs=[seq_blk, sq, sq, sq,
                  pl.BlockSpec((1, PAIRS_PER_BLK, 2, 2, nk + 1, LANES), lambda b, s: (layer, b, 0, 0, 0, 0)),
                  pl.BlockSpec((1, 1, 1, LANES), lambda b, s: (layer, b, 0, 0)),
                  pl.BlockSpec(perm.shape, lambda b, s: (0, 0))],
        out_specs=seq_blk,
        out_shape=jax.ShapeDtypeStruct(u_pad.shape, _F32),
        scratch_shapes=[pltpu.VMEM((m, CHUNK_L * LANES), _BF16),
                        pltpu.VMEM((PAIRS_PER_BLK, m, PAIR_K), _BF16),
                        pltpu.VMEM((2, m, 2 * BLK_STATE), _F32),
                        pltpu.VMEM((m, 4 * BLK_STATE), _BF16),
                        pltpu.VMEM((PAIRS_PER_BLK, m, PAIR_K), _F32),
                        pltpu.VMEM((m, CHUNK_L * LANES), _F32)],
        compiler_params=_params(2),
        name="s5_core",
    )(u_pad, t_mat, w_si, w_so, pw, skip, perm)


def _merge_ffn_kernel(y_ref, ga_ref, gb_ref, x_ref, wglu_ref, wo_ref, npost_ref,
                      npre_ref, w1_ref, w2_ref, npostff_ref, o_ref, *, rows, tiles_per_seg):
    d = D_MODEL
    r0 = pl.multiple_of((pl.program_id(0) % tiles_per_seg) * rows, rows)
    parts = [slice(q * (rows // ROW_PARTS), (q + 1) * (rows // ROW_PARTS)) for q in range(ROW_PARTS)]
    def y_rows(p):
        return jnp.concatenate([y_ref[b, pl.ds(r0 + p.start, p.stop - p.start), :] for b in range(NBLK)], axis=1)

    z = [_gelu(y_rows(p)).astype(_BF16) for p in parts]
    g1 = [_dot(zq, wglu_ref[0, :, 0:d]) for zq in z]
    g2 = [_dot(zq, wglu_ref[0, :, d:2 * d]) for zq in z]
    m = [(ga_ref[p, :].astype(_F32) + gb_ref[p, :].astype(_F32) * (a * _sigmoid(b))).astype(_BF16)
         for p, a, b in zip(parts, g1, g2)]
    x1 = [x_ref[p, :] + _rms(_dot(mq, wo_ref[0]), npost_ref[0]) for p, mq in zip(parts, m)]
    h = [_rms(xq, npre_ref[0]).astype(_BF16) for xq in x1]
    f = [jnp.zeros(xq.shape, _F32) for xq in x1]
    for j in range(D_FF // D_MODEL):
        cs = slice(j * D_MODEL, (j + 1) * D_MODEL)
        for q, hq in enumerate(h):
            r = jnp.maximum(_dot(hq, w1_ref[0, :, cs]), 0.0)
            f[q] = f[q] + _dot((r * r).astype(_BF16), w2_ref[0, cs, :])
    for p, xq, fq in zip(parts, x1, f):
        o_ref[p, :] = xq + _rms(fq, npostff_ref[0])


def _merge_ffn(y_pad, ga, gb, x, w_glu, w_o, npost, npre, w1, w2, npostff, layer, seg):
    ntok = x.shape[0]
    rows = min(TOK_ROWS, seg)
    tps = seg // rows
    pitch = seg + SEG_PAD
    tok = pl.BlockSpec((rows, D_MODEL), lambda i: (i, 0))
    return pl.pallas_call(
        functools.partial(_merge_ffn_kernel, rows=rows, tiles_per_seg=tps),
        grid=(ntok // rows,),
        in_specs=[pl.BlockSpec((NBLK, pitch, LANES), lambda i: (0, i // tps, 0)), tok, tok, tok]
                 + [_layer_spec(w.shape, layer) for w in (w_glu, w_o, npost, npre, w1, w2, npostff)],
        out_specs=tok,
        out_shape=jax.ShapeDtypeStruct((ntok, D_MODEL), _F32),
        compiler_params=pltpu.CompilerParams(dimension_semantics=("arbitrary",),
                                             vmem_limit_bytes=FUSED_VMEM_LIMIT),
        name="merge_ffn",
    )(y_pad, ga, gb, x, w_glu, w_o, npost, npre, w1, w2, npostff)


def _trunk(x, p, s5m):
    nseq, seqlen, d = x.shape
    seg = seqlen // NSEG
    assert d == D_MODEL and seg % GMLP_CHUNK == 0 and seqlen % TOK_ROWS == 0
    x = x.reshape(nseq * seqlen, d)
    t_mat, w_si, w_so, pw = s5m
    for l in range(p["w_in"].shape[0]):
        ga, u_pad, gb = _inproj(x, p["norm_pre_mix"], p["w_in"], p["norm_v"], p["w_s"], p["bias"],
                                p["w_out_a"], l, seg)
        y_pad = _s5_core(u_pad, t_mat, w_si, w_so, pw, p["d_skip"], l, nseq, seg)
        x = _merge_ffn(y_pad, ga, gb, x, p["w_glu"], p["w_o"], p["norm_post_mix"],
                       p["norm_pre_ff"], p["w_ff1"], p["w_ff2"], p["norm_post_ff"], l, seg)
    return x.reshape(nseq, seqlen, d)


def kernel(x_prompt, x_sample, norm_pre_mix, w_in, norm_v, w_s, b_s, w_out_a, lam_re, lam_im,
           log_dt, b_re, b_im, c_re, c_im, d_skip, w_glu, w_o, norm_post_mix, norm_pre_ff,
           w_ff1, w_ff2, norm_post_ff):
    depth = w_in.shape[0]
    bf = lambda w: w.astype(_BF16)
    row = lambda v: v.reshape(depth, 1, D_MODEL)
    p = {
        "norm_pre_mix": row(norm_pre_mix), "w_in": bf(w_in), "norm_v": row(norm_v), "w_s": bf(w_s),
        "bias": jnp.repeat(jnp.swapaxes(b_s, 1, 2), LANES, axis=2),
        "w_out_a": bf(w_out_a), "d_skip": d_skip.reshape(depth, NBLK, 1, LANES),
        "w_glu": bf(w_glu), "w_o": bf(w_o), "norm_post_mix": row(norm_post_mix),
        "norm_pre_ff": row(norm_pre_ff), "w_ff1": bf(w_ff1), "w_ff2": bf(w_ff2),
        "norm_post_ff": row(norm_post_ff),
    }
    outs = []
    mats = {}
    for x in (x_prompt, x_sample):
        nk = x.shape[1] // (NSEG * CHUNK_L)
        if nk not in mats:
            mats[nk] = _s5_matrices(lam_re, lam_im, log_dt, b_re, b_im, c_re, c_im, nk)
        outs.append(_trunk(x, p, mats[nk]))
    return tuple(outs)
```

```python
import functools

import jax
import jax.numpy as jnp
from jax import lax
from jax.experimental import pallas as pl
from jax.experimental.pallas import tpu as pltpu

D_MODEL = 1024
GMLP_CHUNK = 128
A_GROUPS = 8
GROUPS = 64
GROUP_DIM = 16
STATE = 64
D_FF = 4 * D_MODEL
EPS = 1e-6

LANES = 128
SUBLANES = 8
CHUNK_L = 16
NSEG = SUBLANES
NBLK = D_MODEL // LANES
PAIR_LANES = 2 * GROUP_DIM
PAIRS_PER_BLK = LANES // PAIR_LANES
NPAIR = GROUPS // 2
PAIR_K = CHUNK_L * PAIR_LANES
PAIR_STATE = 2 * STATE
BLK_STATE = PAIRS_PER_BLK * PAIR_STATE
SEG_PAD = SUBLANES
ROW_BLK = 2 * SUBLANES
KPB = ROW_BLK // SUBLANES
SCAN_UNROLL = True
TOK_ROWS = 512
ROW_PARTS = 2
VMEM_LIMIT = 56 * 1024 * 1024
FUSED_VMEM_LIMIT = 60 * 1024 * 1024

_F32 = jnp.float32
_BF16 = jnp.bfloat16


def _dot(a, b):
    return jnp.dot(a, b, preferred_element_type=_F32)


def _rms(x, g):
    return x * lax.rsqrt(jnp.mean(x * x, axis=-1, keepdims=True) + EPS) * g


def _sigmoid(x):
    return 1.0 / (1.0 + jnp.exp(-x))


def _gelu(x):
    return jax.nn.gelu(x, approximate=True)


def _layer_spec(shape, layer):
    nd = len(shape)
    return pl.BlockSpec((1,) + tuple(shape[1:]), lambda *_: (layer,) + (0,) * (nd - 1),
                        pipeline_mode=pl.Buffered(1))


def _params(n_axes):
    return pltpu.CompilerParams(
        dimension_semantics=("arbitrary",) * n_axes, vmem_limit_bytes=VMEM_LIMIT)


def _cmul(ar, ai, br, bi):
    return ar * br - ai * bi, ar * bi + ai * br


def _wprep_kernel(lre_ref, lim_ref, ldt_ref, bre_ref, bim_ref, cre_ref, cim_ref,
                  t_ref, wsi_ref, wso_ref, pw_ref,
                  xr_ref, xi_ref, yr_ref, yni_ref, eo_ref, *, nk):
    L = CHUNK_L
    jpos = lax.rem(pl.program_id(1), PAIRS_PER_BLK)
    lane = lax.broadcasted_iota(jnp.int32, (GROUP_DIM, LANES), 1)
    lane1 = lax.broadcasted_iota(jnp.int32, (1, LANES), 1)
    for d in range(2):
        a_pair = []
        for g2 in range(2):
            lre, lim = lre_ref[0, d, g2], lim_ref[0, d, g2]
            dt = jnp.exp(ldt_ref[0, d, g2])
            mag = jnp.exp(lre * dt)
            ar, ai = mag * jnp.cos(lim * dt), mag * jnp.sin(lim * dt)
            a_pair.append((ar, ai))
            den = lre * lre + lim * lim
            qr = ((ar - 1.0) * lre + ai * lim) / den
            qi = (ai * lre - (ar - 1.0) * lim) / den
            n2 = ar * ar + ai * ai
            ir, ii = ar / n2, -ai / n2
            keep = (lane >= g2 * STATE) & (lane < (g2 + 1) * STATE)
            pr, pi = _cmul(qr, qi, jnp.where(keep, bre_ref[0, d, g2], 0.0),
                           jnp.where(keep, bim_ref[0, d, g2], 0.0))
            nr, ni = pr, pi
            fr = jnp.where(keep, cre_ref[0, d, g2], 0.0)
            fi = jnp.where(keep, cim_ref[0, d, g2], 0.0)
            gr, gi = fr, fi

            def rows(k):
                gran = lax.rem(jpos + (k % PAIRS_PER_BLK), PAIRS_PER_BLK)
                start = (k // PAIRS_PER_BLK) * LANES + gran * PAIR_LANES + g2 * GROUP_DIM
                return pl.ds(pl.multiple_of(start, GROUP_DIM), GROUP_DIM)

            for k in range(L + 1):
                if k < L:
                    if d == 0:
                        xr_ref[0, rows(k), :] = nr
                        xi_ref[0, rows(k), :] = ni
                        yr_ref[0, rows(k), :] = fr
                        yni_ref[0, rows(k), :] = -fi
                        wsi_ref[0, 0, rows(L - 1 - k), 0:LANES] = pr.astype(_BF16)
                        wsi_ref[0, 0, rows(L - 1 - k), LANES:2 * LANES] = pi.astype(_BF16)
                    else:
                        xr_ref[1, rows(k), :] = pr
                        xi_ref[1, rows(k), :] = pi
                        yr_ref[1, rows(k), :] = gr
                        yni_ref[1, rows(k), :] = -gi
                        wsi_ref[0, 0, rows(k), 2 * LANES:3 * LANES] = pr.astype(_BF16)
                        wsi_ref[0, 0, rows(k), 3 * LANES:4 * LANES] = pi.astype(_BF16)
                if k >= 1:
                    t = k - 1 if d == 0 else L - k
                    eo_ref[2 * d, rows(t), :] = fr
                    eo_ref[2 * d + 1, rows(t), :] = -fi
                pr, pi = _cmul(ar, ai, pr, pi)
                nr, ni = _cmul(ir, ii, nr, ni)
                fr, fi = _cmul(ar, ai, fr, fi)
                gr, gi = _cmul(ir, ii, gr, gi)

        a_r = jnp.where(lane1 < STATE, a_pair[0][0], a_pair[1][0])
        a_i = jnp.where(lane1 < STATE, a_pair[0][1], a_pair[1][1])
        alr, ali = a_r, a_i
        for _ in range(L - 1):
            alr, ali = _cmul(alr, ali, a_r, a_i)
        one, zero = jnp.ones((1, LANES), _F32), jnp.zeros((1, LANES), _F32)
        pw_ref[0, 0, d, 0, 0:1, :] = one
        pw_ref[0, 0, d, 1, 0:1, :] = zero

        def power(k, carry):
            wr, wi = _cmul(carry[0], carry[1], alr, ali)
            pw_ref[0, 0, d, 0, pl.ds(k + 1, 1), :] = wr
            pw_ref[0, 0, d, 1, pl.ds(k + 1, 1), :] = wi
            return wr, wi

        lax.fori_loop(0, nk, power, (one, zero))

    def split(v):
        hi = v.astype(_BF16)
        return hi, (v - hi.astype(_F32)).astype(_BF16)

    def lag_matrix(d):
        xh, xl = split(jnp.concatenate([xr_ref[d], xi_ref[d]], axis=1))
        yh, yl = split(jnp.concatenate([yr_ref[d], yni_ref[d]], axis=1))
        return lax.dot_general(jnp.concatenate([xh, xh, xl], axis=1), jnp.concatenate([yh, yl, yh], axis=1),
                               (((1,), (1,)), ((), ())), preferred_element_type=_F32)

    def time_of(axis):
        pos = lax.broadcasted_iota(jnp.int32, (PAIR_K, PAIR_K), axis)
        gran = (pos >> (PAIR_LANES.bit_length() - 1)) & (PAIRS_PER_BLK - 1)
        return (pos >> (LANES.bit_length() - 1)) * PAIRS_PER_BLK + ((gran - jpos) & (PAIRS_PER_BLK - 1))

    tau, t = time_of(0), time_of(1)
    t_ref[0, 0] = (jnp.where(t >= tau, lag_matrix(0), 0.0)
                   + jnp.where(tau >= t, lag_matrix(1), 0.0)).astype(_BF16)
    for j in range(4):
        wso_ref[0, 0, j * LANES:(j + 1) * LANES, :] = eo_ref[j].T.astype(_BF16)


def _s5_matrices(lam_re, lam_im, log_dt, b_re, b_im, c_re, c_im, nk):
    depth = lam_re.shape[0]
    dup = lambda v: jnp.concatenate([v, v], axis=-1)
    row = lambda v: dup(v)[:, :, :, None, :]
    ldt = jnp.broadcast_to(log_dt[..., None, None], (depth, 2, GROUPS, 1, LANES))
    bt = lambda v: dup(jnp.swapaxes(v, -1, -2))
    vec = pl.BlockSpec((1, 2, 2, 1, LANES), lambda l, q: (l, 0, q, 0, 0))
    mat = pl.BlockSpec((1, 2, 2, GROUP_DIM, LANES), lambda l, q: (l, 0, q, 0, 0))
    sq = pl.BlockSpec((1, 1, PAIR_K, PAIR_K), lambda l, q: (l, q, 0, 0))
    sq_shape = jax.ShapeDtypeStruct((depth, NPAIR, PAIR_K, PAIR_K), _BF16)
    return pl.pallas_call(
        functools.partial(_wprep_kernel, nk=nk),
        grid=(depth, NPAIR),
        in_specs=[vec, vec, vec, mat, mat, mat, mat],
        out_specs=[sq, sq, sq,
                   pl.BlockSpec((1, 1, 2, 2, nk + 1, LANES), lambda l, q: (l, q, 0, 0, 0, 0))],
        out_shape=[sq_shape, sq_shape, sq_shape,
                   jax.ShapeDtypeStruct((depth, NPAIR, 2, 2, nk + 1, LANES), _F32)],
        scratch_shapes=[pltpu.VMEM((2, PAIR_K, LANES), _F32)] * 4
                       + [pltpu.VMEM((4, PAIR_K, LANES), _F32)],
        compiler_params=_params(2),
        name="s5_weight_prep",
    )(row(lam_re), row(lam_im), ldt, bt(b_re), bt(b_im), dup(c_re), dup(c_im))


def _inproj_kernel(x_ref, npm_ref, win_ref, nv_ref, ws_ref, bs_ref, woa_ref,
                   ga_ref, u_ref, gb_ref, mix_ref, *, rows, tiles_per_seg, seg):
    d = D_MODEL
    prow = rows // ROW_PARTS
    nchunk = prow // GMLP_CHUNK
    parts = [slice(q * prow, (q + 1) * prow) for q in range(ROW_PARTS)]
    h = [_rms(x_ref[p, :], npm_ref[0]).astype(_BF16) for p in parts]
    vn = [_rms(_gelu(_dot(hq, win_ref[0, :, d:2 * d])), nv_ref[0]).astype(_BF16) for hq in h]
    r0 = pl.multiple_of((pl.program_id(0) % tiles_per_seg) * rows, rows)
    for p, hq in zip(parts, h):
        ub = _dot(hq, win_ref[0, :, 2 * d:3 * d])
        for b in range(NBLK):
            u_ref[b, pl.ds(r0 + p.start, prow), :] = ub[:, b * LANES:(b + 1) * LANES]
        gb_ref[p, :] = _sigmoid(_dot(hq, win_ref[0, :, 4 * d:5 * d])).astype(_BF16)
    u_ref[:, seg:seg + SEG_PAD, :] = jnp.zeros((NBLK, SEG_PAD, LANES), _F32)
    ua = [_gelu(_dot(hq, win_ref[0, :, 0:d])) for hq in h]
    for p, vq in zip(parts, vn):
        for g in range(A_GROUPS):
            cs = slice(g * LANES, (g + 1) * LANES)
            v_all = jnp.concatenate([vq[n * GMLP_CHUNK:(n + 1) * GMLP_CHUNK, cs] for n in range(nchunk)], axis=1)
            mixed = _dot(ws_ref[0, g], v_all)
            for n in range(nchunk):
                mix_ref[p.start + n * GMLP_CHUNK:p.start + (n + 1) * GMLP_CHUNK, cs] = (
                    mixed[:, n * LANES:(n + 1) * LANES] + bs_ref[0, :, cs])
    for p, hq, uq in zip(parts, h, ua):
        a = _dot((uq * mix_ref[p, :]).astype(_BF16), woa_ref[0])
        ga_ref[p, :] = (_sigmoid(_dot(hq, win_ref[0, :, 3 * d:4 * d])) * a).astype(_BF16)


def _inproj(x, npm, w_in, nv, w_s, bias, w_out_a, layer, seg):
    ntok = x.shape[0]
    rows = min(TOK_ROWS, seg)
    tps = seg // rows
    pitch = seg + SEG_PAD
    tok = pl.BlockSpec((rows, D_MODEL), lambda i: (i, 0))
    return pl.pallas_call(
        functools.partial(_inproj_kernel, rows=rows, tiles_per_seg=tps, seg=seg),
        grid=(ntok // rows,),
        in_specs=[tok] + [_layer_spec(w.shape, layer) for w in (npm, w_in, nv, w_s, bias, w_out_a)],
        out_specs=[tok, pl.BlockSpec((NBLK, pitch, LANES), lambda i: (0, i // tps, 0)), tok],
        out_shape=[jax.ShapeDtypeStruct((ntok, D_MODEL), _BF16),
                   jax.ShapeDtypeStruct((NBLK, ntok // seg * pitch, LANES), _F32),
                   jax.ShapeDtypeStruct((ntok, D_MODEL), _BF16)],
        scratch_shapes=[pltpu.VMEM((rows, D_MODEL), _F32)],
        compiler_params=_params(1),
        name="inproj_gmlp",
    )(x, npm, w_in, nv, w_s, bias, w_out_a)


def _s5_kernel(u_ref, t_ref, wi_ref, wo_ref, pw_ref, skip_ref, y_ref,
               lhs_ref, z_ref, s_ref, yp_ref, *, nk, pitch):
    ns = BLK_STATE
    npair = PAIRS_PER_BLK
    tgroups = CHUNK_L // npair

    def seg_rows(k, tau):
        return pl.ds(k * CHUNK_L + tau, NSEG, stride=pitch)

    def power_row(d, c, row):
        return jnp.concatenate([pw_ref[0, j, d, c, row, :] for j in range(npair)], axis=1)

    def pick(gran, pieces, base):
        out = pieces[(npair - 1 - base) % npair]
        for g in range(npair - 2, -1, -1):
            out = jnp.where(gran == g, pieces[(g - base) % npair], out)
        return out

    def gather(tg):
        def body(i, _):
            r = pl.ds(pl.multiple_of(i * ROW_BLK, ROW_BLK), ROW_BLK)
            gran = lax.broadcasted_iota(jnp.int32, (ROW_BLK, LANES), 1) >> (PAIR_LANES.bit_length() - 1)
            w = []
            for tl in range(npair):
                v = jnp.concatenate([u_ref[seg_rows(i * KPB + kk, tg * npair + tl), :] for kk in range(KPB)],
                                    axis=0)
                w.append(pltpu.roll(v, tl * PAIR_LANES, 1) if tl else v)
            for j in range(npair):
                lhs_ref[j, r, tg * LANES:(tg + 1) * LANES] = pick(gran, w, j).astype(_BF16)
            return 0

        lax.fori_loop(0, nk // KPB, body, 0, unroll=True)

    for tg in range(tgroups):
        gather(tg)

    seg_id = lax.broadcasted_iota(jnp.int32, (NSEG, ns), 0)
    for d in range(2):
        for j in range(PAIRS_PER_BLK):
            zz = _dot(lhs_ref[j], wi_ref[0, j, :, d * 2 * LANES:(d + 1) * 2 * LANES])
            z_ref[d, :, j * LANES:(j + 1) * LANES] = zz[:, 0:LANES]
            z_ref[d, :, ns + j * LANES:ns + (j + 1) * LANES] = zz[:, LANES:2 * LANES]
    for j in range(PAIRS_PER_BLK):
        yp_ref[j] = _dot(lhs_ref[j], t_ref[0, j])

    for d in range(2):
        backward = d == 1
        al_r = jnp.broadcast_to(power_row(d, 0, slice(1, 2)), (NSEG, ns))
        al_i = jnp.broadcast_to(power_row(d, 1, slice(1, 2)), (NSEG, ns))

        def local_scan(j, carry):
            k = nk - 1 - j if backward else j
            r = pl.ds(pl.multiple_of(k * SUBLANES, SUBLANES), SUBLANES)
            sr, si = carry
            zr, zi = z_ref[d, r, 0:ns], z_ref[d, r, ns:2 * ns]
            z_ref[d, r, 0:ns] = sr
            z_ref[d, r, ns:2 * ns] = si
            pr, pi = _cmul(al_r, al_i, sr, si)
            return pr + zr, pi + zi

        zero = jnp.zeros((NSEG, ns), _F32)
        tot_r, tot_i = lax.fori_loop(0, nk, local_scan, (zero, zero), unroll=SCAN_UNROLL)

        as_r = jnp.broadcast_to(power_row(d, 0, slice(nk, nk + 1)), (NSEG, ns))
        as_i = jnp.broadcast_to(power_row(d, 1, slice(nk, nk + 1)), (NSEG, ns))
        edge = NSEG - 1 if backward else 0
        shift = NSEG - 1 if backward else 1
        cr, ci = zero, zero
        for _ in range(NSEG - 1):
            pr, pi = _cmul(as_r, as_i, cr, ci)
            cr = jnp.where(seg_id == edge, 0.0, pltpu.roll(pr + tot_r, shift, 0))
            ci = jnp.where(seg_id == edge, 0.0, pltpu.roll(pi + tot_i, shift, 0))

        def add_carry(i, _):
            r = pl.ds(pl.multiple_of(i * ROW_BLK, ROW_BLK), ROW_BLK)
            sr, si = [], []
            for kk in range(KPB):
                k = i * KPB + kk
                e = nk - 1 - k if backward else k
                pr, pi = _cmul(power_row(d, 0, pl.ds(e, 1)), power_row(d, 1, pl.ds(e, 1)), cr, ci)
                rk = pl.ds(pl.multiple_of(k * SUBLANES, SUBLANES), SUBLANES)
                sr.append(z_ref[d, rk, 0:ns] + pr)
                si.append(z_ref[d, rk, ns:2 * ns] + pi)
            s_ref[r, d * 2 * ns:d * 2 * ns + ns] = jnp.concatenate(sr, axis=0).astype(_BF16)
            s_ref[r, d * 2 * ns + ns:(d + 1) * 2 * ns] = jnp.concatenate(si, axis=0).astype(_BF16)
            return 0

        lax.fori_loop(0, nk // KPB, add_carry, 0, unroll=SCAN_UNROLL)

    for j in range(PAIRS_PER_BLK):
        sp = jnp.concatenate([s_ref[:, q * ns + j * LANES:q * ns + (j + 1) * LANES] for q in range(4)],
                             axis=1)
        yp_ref[j] = yp_ref[j] + _dot(sp, wo_ref[0, j])

    skip = skip_ref[0, 0]

    def scatter(tg):
        def body(k, _):
            r = pl.ds(pl.multiple_of(k * SUBLANES, SUBLANES), SUBLANES)
            gran = lax.broadcasted_iota(jnp.int32, (SUBLANES, LANES), 1) >> (PAIR_LANES.bit_length() - 1)
            yb = [yp_ref[j, r, tg * LANES:(tg + 1) * LANES] for j in range(npair)]
            for tl in range(npair):
                v = pick(gran, yb, tl)
                v = pltpu.roll(v, LANES - tl * PAIR_LANES, 1) if tl else v
                rows = seg_rows(k, tg * npair + tl)
                y_ref[rows, :] = v + skip * u_ref[rows, :]
            return 0

        lax.fori_loop(0, nk, body, 0, unroll=True)

    for tg in range(tgroups):
        scatter(tg)
    seg = nk * CHUNK_L
    for s in range(NSEG):
        y_ref[s * pitch + seg:(s + 1) * pitch, :] = jnp.zeros((SEG_PAD, LANES), _F32)


def _s5_core(u_pad, t_mat, w_si, w_so, pw, skip, layer, nseq, seg):
    nk = seg // CHUNK_L
    pitch = seg + SEG_PAD
    m = nk * NSEG
    seq_blk = pl.BlockSpec((None, NSEG * pitch, LANES), lambda b, s: (b, s, 0))
    sq = pl.BlockSpec((1, PAIRS_PER_BLK, PAIR_K, PAIR_K), lambda b, s: (layer, b, 0, 0))
    return pl.pallas_call(
        functools.partial(_s5_kernel, nk=nk, pitch=pitch),
        grid=(NBLK, nseq),
        in_specs=[seq_blk, sq, sq, sq,
                  pl.BlockSpec((1, PAIRS_PER_BLK, 2, 2, nk + 1, LANES), lambda b, s: (layer, b, 0, 0, 0, 0)),
                  pl.BlockSpec((1, 1, 1, LANES), lambda b, s: (layer, b, 0, 0))],
        out_specs=seq_blk,
        out_shape=jax.ShapeDtypeStruct(u_pad.shape, _F32),
        scratch_shapes=[pltpu.VMEM((PAIRS_PER_BLK, m, PAIR_K), _BF16),
                        pltpu.VMEM((2, m, 2 * BLK_STATE), _F32),
                        pltpu.VMEM((m, 4 * BLK_STATE), _BF16),
                        pltpu.VMEM((PAIRS_PER_BLK, m, PAIR_K), _F32)],
        compiler_params=_params(2),
        name="s5_core",
    )(u_pad, t_mat, w_si, w_so, pw, skip)


def _merge_ffn_kernel(y_ref, ga_ref, gb_ref, x_ref, wglu_ref, wo_ref, npost_ref,
                      npre_ref, w1_ref, w2_ref, npostff_ref, o_ref, *, rows, tiles_per_seg):
    d = D_MODEL
    r0 = pl.multiple_of((pl.program_id(0) % tiles_per_seg) * rows, rows)
    parts = [slice(q * (rows // ROW_PARTS), (q + 1) * (rows // ROW_PARTS)) for q in range(ROW_PARTS)]
    def y_rows(p):
        return jnp.concatenate([y_ref[b, pl.ds(r0 + p.start, p.stop - p.start), :] for b in range(NBLK)], axis=1)

    z = [_gelu(y_rows(p)).astype(_BF16) for p in parts]
    g1 = [_dot(zq, wglu_ref[0, :, 0:d]) for zq in z]
    g2 = [_dot(zq, wglu_ref[0, :, d:2 * d]) for zq in z]
    m = [(ga_ref[p, :].astype(_F32) + gb_ref[p, :].astype(_F32) * (a * _sigmoid(b))).astype(_BF16)
         for p, a, b in zip(parts, g1, g2)]
    x1 = [x_ref[p, :] + _rms(_dot(mq, wo_ref[0]), npost_ref[0]) for p, mq in zip(parts, m)]
    h = [_rms(xq, npre_ref[0]).astype(_BF16) for xq in x1]
    f = [jnp.zeros(xq.shape, _F32) for xq in x1]
    for j in range(D_FF // D_MODEL):
        cs = slice(j * D_MODEL, (j + 1) * D_MODEL)
        for q, hq in enumerate(h):
            r = jnp.maximum(_dot(hq, w1_ref[0, :, cs]), 0.0)
            f[q] = f[q] + _dot((r * r).astype(_BF16), w2_ref[0, cs, :])
    for p, xq, fq in zip(parts, x1, f):
        o_ref[p, :] = xq + _rms(fq, npostff_ref[0])


def _merge_ffn(y_pad, ga, gb, x, w_glu, w_o, npost, npre, w1, w2, npostff, layer, seg):
    ntok = x.shape[0]
    rows = min(TOK_ROWS, seg)
    tps = seg // rows
    pitch = seg + SEG_PAD
    tok = pl.BlockSpec((rows, D_MODEL), lambda i: (i, 0))
    return pl.pallas_call(
        functools.partial(_merge_ffn_kernel, rows=rows, tiles_per_seg=tps),
        grid=(ntok // rows,),
        in_specs=[pl.BlockSpec((NBLK, pitch, LANES), lambda i: (0, i // tps, 0)), tok, tok, tok]
                 + [_layer_spec(w.shape, layer) for w in (w_glu, w_o, npost, npre, w1, w2, npostff)],
        out_specs=tok,
        out_shape=jax.ShapeDtypeStruct((ntok, D_MODEL), _F32),
        compiler_params=pltpu.CompilerParams(dimension_semantics=("arbitrary",),
                                             vmem_limit_bytes=FUSED_VMEM_LIMIT),
        name="merge_ffn",
    )(y_pad, ga, gb, x, w_glu, w_o, npost, npre, w1, w2, npostff)


def _trunk(x, p, s5m):
    nseq, seqlen, d = x.shape
    seg = seqlen // NSEG
    assert d == D_MODEL and seg % GMLP_CHUNK == 0 and seqlen % TOK_ROWS == 0
    x = x.reshape(nseq * seqlen, d)
    t_mat, w_si, w_so, pw = s5m
    for l in range(p["w_in"].shape[0]):
        ga, u_pad, gb = _inproj(x, p["norm_pre_mix"], p["w_in"], p["norm_v"], p["w_s"], p["bias"],
                                p["w_out_a"], l, seg)
        y_pad = _s5_core(u_pad, t_mat, w_si, w_so, pw, p["d_skip"], l, nseq, seg)
        x = _merge_ffn(y_pad, ga, gb, x, p["w_glu"], p["w_o"], p["norm_post_mix"],
                       p["norm_pre_ff"], p["w_ff1"], p["w_ff2"], p["norm_post_ff"], l, seg)
    return x.reshape(nseq, seqlen, d)


def kernel(x_prompt, x_sample, norm_pre_mix, w_in, norm_v, w_s, b_s, w_out_a, lam_re, lam_im,
           log_dt, b_re, b_im, c_re, c_im, d_skip, w_glu, w_o, norm_post_mix, norm_pre_ff,
           w_ff1, w_ff2, norm_post_ff):
    depth = w_in.shape[0]
    bf = lambda w: w.astype(_BF16)
    row = lambda v: v.reshape(depth, 1, D_MODEL)
    p = {
        "norm_pre_mix": row(norm_pre_mix), "w_in": bf(w_in), "norm_v": row(norm_v), "w_s": bf(w_s),
        "bias": jnp.repeat(jnp.swapaxes(b_s, 1, 2), LANES, axis=2),
        "w_out_a": bf(w_out_a), "d_skip": d_skip.reshape(depth, NBLK, 1, LANES),
        "w_glu": bf(w_glu), "w_o": bf(w_o), "norm_post_mix": row(norm_post_mix),
        "norm_pre_ff": row(norm_pre_ff), "w_ff1": bf(w_ff1), "w_ff2": bf(w_ff2),
        "norm_post_ff": row(norm_post_ff),
    }
    outs = []
    mats = {}
    for x in (x_prompt, x_sample):
        nk = x.shape[1] // (NSEG * CHUNK_L)
        if nk not in mats:
            mats[nk] = _s5_matrices(lam_re, lam_im, log_dt, b_re, b_im, c_re, c_im, nk)
        outs.append(_trunk(x, p, mats[nk]))
    return tuple(outs)
```

```python
import functools

import jax
import jax.numpy as jnp
from jax import lax
from jax.experimental import pallas as pl
from jax.experimental.pallas import tpu as pltpu

D_MODEL = 1024
GMLP_CHUNK = 128
A_GROUPS = 8
GROUPS = 64
GROUP_DIM = 16
STATE = 64
D_FF = 4 * D_MODEL
EPS = 1e-6

LANES = 128
SUBLANES = 8
CHUNK_L = 16
NSEG = SUBLANES
NBLK = D_MODEL // LANES
PAIR_LANES = 2 * GROUP_DIM
PAIRS_PER_BLK = LANES // PAIR_LANES
NPAIR = GROUPS // 2
PAIR_K = CHUNK_L * PAIR_LANES
PAIR_STATE = 2 * STATE
BLK_STATE = PAIRS_PER_BLK * PAIR_STATE
SEG_PAD = SUBLANES
ROW_BLK = 2 * SUBLANES
KPB = ROW_BLK // SUBLANES
SCAN_UNROLL = True
PREP_PAIRS = 2
TOK_ROWS = 512
ROW_PARTS = 2
VMEM_LIMIT = 56 * 1024 * 1024
FUSED_VMEM_LIMIT = 60 * 1024 * 1024

_F32 = jnp.float32
_BF16 = jnp.bfloat16


def _dot(a, b):
    return jnp.dot(a, b, preferred_element_type=_F32)


def _rms(x, g):
    return x * lax.rsqrt(jnp.mean(x * x, axis=-1, keepdims=True) + EPS) * g


def _sigmoid(x):
    return 1.0 / (1.0 + jnp.exp(-x))


def _gelu(x):
    return jax.nn.gelu(x, approximate=True)


def _layer_spec(shape, layer):
    nd = len(shape)
    return pl.BlockSpec((1,) + tuple(shape[1:]), lambda *_: (layer,) + (0,) * (nd - 1),
                        pipeline_mode=pl.Buffered(1))


def _params(n_axes):
    return pltpu.CompilerParams(
        dimension_semantics=("arbitrary",) * n_axes, vmem_limit_bytes=VMEM_LIMIT)


def _cmul(ar, ai, br, bi):
    return ar * br - ai * bi, ar * bi + ai * br


def _wprep_pair(lre_ref, lim_ref, ldt_ref, bre_ref, bim_ref, cre_ref, cim_ref,
                  t_ref, wsi_ref, wso_ref, pw_ref,
                  xr_ref, xi_ref, yr_ref, yni_ref, eo_ref, *, nk, q):
    L = CHUNK_L
    jpos = lax.rem(pl.program_id(1) * PREP_PAIRS + q, PAIRS_PER_BLK)
    lane = lax.broadcasted_iota(jnp.int32, (GROUP_DIM, LANES), 1)
    lane1 = lax.broadcasted_iota(jnp.int32, (1, LANES), 1)
    for d in range(2):
        a_pair = []
        for g2 in range(2):
            lre, lim = lre_ref[0, d, 2 * q + g2], lim_ref[0, d, 2 * q + g2]
            dt = jnp.exp(ldt_ref[0, d, 2 * q + g2])
            mag = jnp.exp(lre * dt)
            ar, ai = mag * jnp.cos(lim * dt), mag * jnp.sin(lim * dt)
            a_pair.append((ar, ai))
            den = lre * lre + lim * lim
            qr = ((ar - 1.0) * lre + ai * lim) / den
            qi = (ai * lre - (ar - 1.0) * lim) / den
            n2 = ar * ar + ai * ai
            ir, ii = ar / n2, -ai / n2
            keep = (lane >= g2 * STATE) & (lane < (g2 + 1) * STATE)
            pr, pi = _cmul(qr, qi, jnp.where(keep, bre_ref[0, d, 2 * q + g2], 0.0),
                           jnp.where(keep, bim_ref[0, d, 2 * q + g2], 0.0))
            nr, ni = pr, pi
            fr = jnp.where(keep, cre_ref[0, d, 2 * q + g2], 0.0)
            fi = jnp.where(keep, cim_ref[0, d, 2 * q + g2], 0.0)
            gr, gi = fr, fi

            def rows(k):
                gran = lax.rem(jpos + (k % PAIRS_PER_BLK), PAIRS_PER_BLK)
                start = (k // PAIRS_PER_BLK) * LANES + gran * PAIR_LANES + g2 * GROUP_DIM
                return pl.ds(pl.multiple_of(start, GROUP_DIM), GROUP_DIM)

            for k in range(L + 1):
                if k < L:
                    if d == 0:
                        xr_ref[2 * q, rows(k), :] = nr
                        xi_ref[2 * q, rows(k), :] = ni
                        yr_ref[2 * q, rows(k), :] = fr
                        yni_ref[2 * q, rows(k), :] = -fi
                        wsi_ref[0, q, rows(L - 1 - k), 0:LANES] = pr.astype(_BF16)
                        wsi_ref[0, q, rows(L - 1 - k), LANES:2 * LANES] = pi.astype(_BF16)
                    else:
                        xr_ref[2 * q + 1, rows(k), :] = pr
                        xi_ref[2 * q + 1, rows(k), :] = pi
                        yr_ref[2 * q + 1, rows(k), :] = gr
                        yni_ref[2 * q + 1, rows(k), :] = -gi
                        wsi_ref[0, q, rows(k), 2 * LANES:3 * LANES] = pr.astype(_BF16)
                        wsi_ref[0, q, rows(k), 3 * LANES:4 * LANES] = pi.astype(_BF16)
                if k >= 1:
                    t = k - 1 if d == 0 else L - k
                    eo_ref[4 * q + 2 * d, rows(t), :] = fr
                    eo_ref[4 * q + 2 * d + 1, rows(t), :] = -fi
                pr, pi = _cmul(ar, ai, pr, pi)
                nr, ni = _cmul(ir, ii, nr, ni)
                fr, fi = _cmul(ar, ai, fr, fi)
                gr, gi = _cmul(ir, ii, gr, gi)

        a_r = jnp.where(lane1 < STATE, a_pair[0][0], a_pair[1][0])
        a_i = jnp.where(lane1 < STATE, a_pair[0][1], a_pair[1][1])
        alr, ali = a_r, a_i
        for _ in range(L - 1):
            alr, ali = _cmul(alr, ali, a_r, a_i)
        one, zero = jnp.ones((1, LANES), _F32), jnp.zeros((1, LANES), _F32)
        pw_ref[0, q, d, 0, 0:1, :] = one
        pw_ref[0, q, d, 1, 0:1, :] = zero

        def power(k, carry):
            wr, wi = _cmul(carry[0], carry[1], alr, ali)
            pw_ref[0, q, d, 0, pl.ds(k + 1, 1), :] = wr
            pw_ref[0, q, d, 1, pl.ds(k + 1, 1), :] = wi
            return wr, wi

        lax.fori_loop(0, nk, power, (one, zero), unroll=True)

    def split(v):
        hi = v.astype(_BF16)
        return hi, (v - hi.astype(_F32)).astype(_BF16)

    def lag_matrix(d):
        xh, xl = split(jnp.concatenate([xr_ref[2 * q + d], xi_ref[2 * q + d]], axis=1))
        yh, yl = split(jnp.concatenate([yr_ref[2 * q + d], yni_ref[2 * q + d]], axis=1))
        return lax.dot_general(jnp.concatenate([xh, xh, xl], axis=1), jnp.concatenate([yh, yl, yh], axis=1),
                               (((1,), (1,)), ((), ())), preferred_element_type=_F32)

    def time_of(axis):
        pos = lax.broadcasted_iota(jnp.int32, (PAIR_K, PAIR_K), axis)
        gran = (pos >> (PAIR_LANES.bit_length() - 1)) & (PAIRS_PER_BLK - 1)
        return (pos >> (LANES.bit_length() - 1)) * PAIRS_PER_BLK + ((gran - jpos) & (PAIRS_PER_BLK - 1))

    tau, t = time_of(0), time_of(1)
    t_ref[0, q] = (jnp.where(t >= tau, lag_matrix(0), 0.0)
                   + jnp.where(tau >= t, lag_matrix(1), 0.0)).astype(_BF16)
    for j in range(4):
        wso_ref[0, q, j * LANES:(j + 1) * LANES, :] = eo_ref[4 * q + j].T.astype(_BF16)


def _wprep_kernel(*refs, nk):
    for q in range(PREP_PAIRS):
        _wprep_pair(*refs, nk=nk, q=q)


def _s5_matrices(lam_re, lam_im, log_dt, b_re, b_im, c_re, c_im, nk):
    depth = lam_re.shape[0]
    dup = lambda v: jnp.concatenate([v, v], axis=-1)
    row = lambda v: dup(v)[:, :, :, None, :]
    ldt = jnp.broadcast_to(log_dt[..., None, None], (depth, 2, GROUPS, 1, LANES))
    bt = lambda v: dup(jnp.swapaxes(v, -1, -2))
    vec = pl.BlockSpec((1, 2, 2 * PREP_PAIRS, 1, LANES), lambda l, q: (l, 0, q, 0, 0))
    mat = pl.BlockSpec((1, 2, 2 * PREP_PAIRS, GROUP_DIM, LANES), lambda l, q: (l, 0, q, 0, 0))
    sq = pl.BlockSpec((1, PREP_PAIRS, PAIR_K, PAIR_K), lambda l, q: (l, q, 0, 0))
    sq_shape = jax.ShapeDtypeStruct((depth, NPAIR, PAIR_K, PAIR_K), _BF16)
    return pl.pallas_call(
        functools.partial(_wprep_kernel, nk=nk),
        grid=(depth, NPAIR // PREP_PAIRS),
        in_specs=[vec, vec, vec, mat, mat, mat, mat],
        out_specs=[sq, sq, sq,
                   pl.BlockSpec((1, PREP_PAIRS, 2, 2, nk + 1, LANES), lambda l, q: (l, q, 0, 0, 0, 0))],
        out_shape=[sq_shape, sq_shape, sq_shape,
                   jax.ShapeDtypeStruct((depth, NPAIR, 2, 2, nk + 1, LANES), _F32)],
        scratch_shapes=[pltpu.VMEM((2 * PREP_PAIRS, PAIR_K, LANES), _F32)] * 4
                       + [pltpu.VMEM((4 * PREP_PAIRS, PAIR_K, LANES), _F32)],
        compiler_params=_params(2),
        name="s5_weight_prep",
    )(row(lam_re), row(lam_im), ldt, bt(b_re), bt(b_im), dup(c_re), dup(c_im))


def _inproj_kernel(x_ref, npm_ref, win_ref, nv_ref, ws_ref, bs_ref, woa_ref,
                   ga_ref, u_ref, gb_ref, mix_ref, *, rows, tiles_per_seg, seg):
    d = D_MODEL
    prow = rows // ROW_PARTS
    nchunk = prow // GMLP_CHUNK
    parts = [slice(q * prow, (q + 1) * prow) for q in range(ROW_PARTS)]
    h = [_rms(x_ref[p, :], npm_ref[0]).astype(_BF16) for p in parts]
    vn = [_rms(_gelu(_dot(hq, win_ref[0, :, d:2 * d])), nv_ref[0]).astype(_BF16) for hq in h]
    r0 = pl.multiple_of((pl.program_id(0) % tiles_per_seg) * rows, rows)
    for p, hq in zip(parts, h):
        ub = _dot(hq, win_ref[0, :, 2 * d:3 * d])
        for b in range(NBLK):
            u_ref[b, pl.ds(r0 + p.start, prow), :] = ub[:, b * LANES:(b + 1) * LANES]
        gb_ref[p, :] = _sigmoid(_dot(hq, win_ref[0, :, 4 * d:5 * d])).astype(_BF16)
    u_ref[:, seg:seg + SEG_PAD, :] = jnp.zeros((NBLK, SEG_PAD, LANES), _F32)
    ua = [_gelu(_dot(hq, win_ref[0, :, 0:d])) for hq in h]
    for p, vq in zip(parts, vn):
        for g in range(A_GROUPS):
            cs = slice(g * LANES, (g + 1) * LANES)
            v_all = jnp.concatenate([vq[n * GMLP_CHUNK:(n + 1) * GMLP_CHUNK, cs] for n in range(nchunk)], axis=1)
            mixed = _dot(ws_ref[0, g], v_all)
            for n in range(nchunk):
                mix_ref[p.start + n * GMLP_CHUNK:p.start + (n + 1) * GMLP_CHUNK, cs] = (
                    mixed[:, n * LANES:(n + 1) * LANES] + bs_ref[0, :, cs])
    for p, hq, uq in zip(parts, h, ua):
        a = _dot((uq * mix_ref[p, :]).astype(_BF16), woa_ref[0])
        ga_ref[p, :] = (_sigmoid(_dot(hq, win_ref[0, :, 3 * d:4 * d])) * a).astype(_BF16)


def _inproj(x, npm, w_in, nv, w_s, bias, w_out_a, layer, seg):
    ntok = x.shape[0]
    rows = min(TOK_ROWS, seg)
    tps = seg // rows
    pitch = seg + SEG_PAD
    tok = pl.BlockSpec((rows, D_MODEL), lambda i: (i, 0))
    return pl.pallas_call(
        functools.partial(_inproj_kernel, rows=rows, tiles_per_seg=tps, seg=seg),
        grid=(ntok // rows,),
        in_specs=[tok] + [_layer_spec(w.shape, layer) for w in (npm, w_in, nv, w_s, bias, w_out_a)],
        out_specs=[tok, pl.BlockSpec((NBLK, pitch, LANES), lambda i: (0, i // tps, 0)), tok],
        out_shape=[jax.ShapeDtypeStruct((ntok, D_MODEL), _BF16),
                   jax.ShapeDtypeStruct((NBLK, ntok // seg * pitch, LANES), _F32),
                   jax.ShapeDtypeStruct((ntok, D_MODEL), _BF16)],
        scratch_shapes=[pltpu.VMEM((rows, D_MODEL), _F32)],
        compiler_params=_params(1),
        name="inproj_gmlp",
    )(x, npm, w_in, nv, w_s, bias, w_out_a)


def _s5_kernel(u_ref, t_ref, wi_ref, wo_ref, pw_ref, skip_ref, y_ref,
               lhs_ref, z_ref, s_ref, yp_ref, *, nk, pitch):
    ns = BLK_STATE
    npair = PAIRS_PER_BLK
    tgroups = CHUNK_L // npair

    def seg_rows(k, tau):
        return pl.ds(k * CHUNK_L + tau, NSEG, stride=pitch)

    def power_row(d, c, row):
        return jnp.concatenate([pw_ref[0, j, d, c, row, :] for j in range(npair)], axis=1)

    def pick(gran, pieces, base):
        out = pieces[(npair - 1 - base) % npair]
        for g in range(npair - 2, -1, -1):
            out = jnp.where(gran == g, pieces[(g - base) % npair], out)
        return out

    def gather(tg):
        def body(i, _):
            r = pl.ds(pl.multiple_of(i * ROW_BLK, ROW_BLK), ROW_BLK)
            gran = lax.broadcasted_iota(jnp.int32, (ROW_BLK, LANES), 1) >> (PAIR_LANES.bit_length() - 1)
            w = []
            for tl in range(npair):
                v = jnp.concatenate([u_ref[seg_rows(i * KPB + kk, tg * npair + tl), :] for kk in range(KPB)],
                                    axis=0)
                w.append(pltpu.roll(v, tl * PAIR_LANES, 1) if tl else v)
            for j in range(npair):
                lhs_ref[j, r, tg * LANES:(tg + 1) * LANES] = pick(gran, w, j).astype(_BF16)
            return 0

        lax.fori_loop(0, nk // KPB, body, 0, unroll=True)

    for tg in range(tgroups):
        gather(tg)

    seg_id = lax.broadcasted_iota(jnp.int32, (NSEG, ns), 0)
    for d in range(2):
        for j in range(PAIRS_PER_BLK):
            zz = _dot(lhs_ref[j], wi_ref[0, j, :, d * 2 * LANES:(d + 1) * 2 * LANES])
            z_ref[d, :, j * LANES:(j + 1) * LANES] = zz[:, 0:LANES]
            z_ref[d, :, ns + j * LANES:ns + (j + 1) * LANES] = zz[:, LANES:2 * LANES]
    for j in range(PAIRS_PER_BLK):
        yp_ref[j] = _dot(lhs_ref[j], t_ref[0, j])

    for d in range(2):
        backward = d == 1
        al_r = jnp.broadcast_to(power_row(d, 0, slice(1, 2)), (NSEG, ns))
        al_i = jnp.broadcast_to(power_row(d, 1, slice(1, 2)), (NSEG, ns))

        def local_scan(j, carry):
            k = nk - 1 - j if backward else j
            r = pl.ds(pl.multiple_of(k * SUBLANES, SUBLANES), SUBLANES)
            sr, si = carry
            zr, zi = z_ref[d, r, 0:ns], z_ref[d, r, ns:2 * ns]
            z_ref[d, r, 0:ns] = sr
            z_ref[d, r, ns:2 * ns] = si
            pr, pi = _cmul(al_r, al_i, sr, si)
            return pr + zr, pi + zi

        zero = jnp.zeros((NSEG, ns), _F32)
        tot_r, tot_i = lax.fori_loop(0, nk, local_scan, (zero, zero), unroll=SCAN_UNROLL)

        as_r = jnp.broadcast_to(power_row(d, 0, slice(nk, nk + 1)), (NSEG, ns))
        as_i = jnp.broadcast_to(power_row(d, 1, slice(nk, nk + 1)), (NSEG, ns))
        edge = NSEG - 1 if backward else 0
        shift = NSEG - 1 if backward else 1
        cr, ci = zero, zero
        for _ in range(NSEG - 1):
            pr, pi = _cmul(as_r, as_i, cr, ci)
            cr = jnp.where(seg_id == edge, 0.0, pltpu.roll(pr + tot_r, shift, 0))
            ci = jnp.where(seg_id == edge, 0.0, pltpu.roll(pi + tot_i, shift, 0))

        def add_carry(i, _):
            r = pl.ds(pl.multiple_of(i * ROW_BLK, ROW_BLK), ROW_BLK)
            sr, si = [], []
            for kk in range(KPB):
                k = i * KPB + kk
                e = nk - 1 - k if backward else k
                pr, pi = _cmul(power_row(d, 0, pl.ds(e, 1)), power_row(d, 1, pl.ds(e, 1)), cr, ci)
                rk = pl.ds(pl.multiple_of(k * SUBLANES, SUBLANES), SUBLANES)
                sr.append(z_ref[d, rk, 0:ns] + pr)
                si.append(z_ref[d, rk, ns:2 * ns] + pi)
            s_ref[r, d * 2 * ns:d * 2 * ns + ns] = jnp.concatenate(sr, axis=0).astype(_BF16)
            s_ref[r, d * 2 * ns + ns:(d + 1) * 2 * ns] = jnp.concatenate(si, axis=0).astype(_BF16)
            return 0

        lax.fori_loop(0, nk // KPB, add_carry, 0, unroll=SCAN_UNROLL)

    for j in range(PAIRS_PER_BLK):
        sp = jnp.concatenate([s_ref[:, q * ns + j * LANES:q * ns + (j + 1) * LANES] for q in range(4)],
                             axis=1)
        yp_ref[j] = yp_ref[j] + _dot(sp, wo_ref[0, j])

    skip = skip_ref[0, 0]

    def scatter(tg):
        def body(k, _):
            r = pl.ds(pl.multiple_of(k * SUBLANES, SUBLANES), SUBLANES)
            gran = lax.broadcasted_iota(jnp.int32, (SUBLANES, LANES), 1) >> (PAIR_LANES.bit_length() - 1)
            yb = [yp_ref[j, r, tg * LANES:(tg + 1) * LANES] for j in range(npair)]
            for tl in range(npair):
                v = pick(gran, yb, tl)
                v = pltpu.roll(v, LANES - tl * PAIR_LANES, 1) if tl else v
                rows = seg_rows(k, tg * npair + tl)
                y_ref[rows, :] = v + skip * u_ref[rows, :]
            return 0

        lax.fori_loop(0, nk, body, 0, unroll=True)

    for tg in range(tgroups):
        scatter(tg)
    seg = nk * CHUNK_L
    for s in range(NSEG):
        y_ref[s * pitch + seg:(s + 1) * pitch, :] = jnp.zeros((SEG_PAD, LANES), _F32)


def _s5_core(u_pad, t_mat, w_si, w_so, pw, skip, layer, nseq, seg):
    nk = seg // CHUNK_L
    pitch = seg + SEG_PAD
    m = nk * NSEG
    seq_blk = pl.BlockSpec((None, NSEG * pitch, LANES), lambda b, s: (b, s, 0))
    sq = pl.BlockSpec((1, PAIRS_PER_BLK, PAIR_K, PAIR_K), lambda b, s: (layer, b, 0, 0))
    return pl.pallas_call(
        functools.partial(_s5_kernel, nk=nk, pitch=pitch),
        grid=(NBLK, nseq),
        in_specs=[seq_blk, sq, sq, sq,
                  pl.BlockSpec((1, PAIRS_PER_BLK, 2, 2, nk + 1, LANES), lambda b, s: (layer, b, 0, 0, 0, 0)),
                  pl.BlockSpec((1, 1, 1, LANES), lambda b, s: (layer, b, 0, 0))],
        out_specs=seq_blk,
        out_shape=jax.ShapeDtypeStruct(u_pad.shape, _F32),
        scratch_shapes=[pltpu.VMEM((PAIRS_PER_BLK, m, PAIR_K), _BF16),
                        pltpu.VMEM((2, m, 2 * BLK_STATE), _F32),
                        pltpu.VMEM((m, 4 * BLK_STATE), _BF16),
                        pltpu.VMEM((PAIRS_PER_BLK, m, PAIR_K), _F32)],
        compiler_params=_params(2),
        name="s5_core",
    )(u_pad, t_mat, w_si, w_so, pw, skip)


def _merge_ffn_kernel(y_ref, ga_ref, gb_ref, x_ref, wglu_ref, wo_ref, npost_ref,
                      npre_ref, w1_ref, w2_ref, npostff_ref, o_ref, *, rows, tiles_per_seg):
    d = D_MODEL
    r0 = pl.multiple_of((pl.program_id(0) % tiles_per_seg) * rows, rows)
    parts = [slice(q * (rows // ROW_PARTS), (q + 1) * (rows // ROW_PARTS)) for q in range(ROW_PARTS)]
    def y_rows(p):
        return jnp.concatenate([y_ref[b, pl.ds(r0 + p.start, p.stop - p.start), :] for b in range(NBLK)], axis=1)

    z = [_gelu(y_rows(p)).astype(_BF16) for p in parts]
    g1 = [_dot(zq, wglu_ref[0, :, 0:d]) for zq in z]
    g2 = [_dot(zq, wglu_ref[0, :, d:2 * d]) for zq in z]
    m = [(ga_ref[p, :].astype(_F32) + gb_ref[p, :].astype(_F32) * (a * _sigmoid(b))).astype(_BF16)
         for p, a, b in zip(parts, g1, g2)]
    x1 = [x_ref[p, :] + _rms(_dot(mq, wo_ref[0]), npost_ref[0]) for p, mq in zip(parts, m)]
    h = [_rms(xq, npre_ref[0]).astype(_BF16) for xq in x1]
    f = [jnp.zeros(xq.shape, _F32) for xq in x1]
    for j in range(D_FF // D_MODEL):
        cs = slice(j * D_MODEL, (j + 1) * D_MODEL)
        for q, hq in enumerate(h):
            r = jnp.maximum(_dot(hq, w1_ref[0, :, cs]), 0.0)
            f[q] = f[q] + _dot((r * r).astype(_BF16), w2_ref[0, cs, :])
    for p, xq, fq in zip(parts, x1, f):
        o_ref[p, :] = xq + _rms(fq, npostff_ref[0])


def _merge_ffn(y_pad, ga, gb, x, w_glu, w_o, npost, npre, w1, w2, npostff, layer, seg):
    ntok = x.shape[0]
    rows = min(TOK_ROWS, seg)
    tps = seg // rows
    pitch = seg + SEG_PAD
    tok = pl.BlockSpec((rows, D_MODEL), lambda i: (i, 0))
    return pl.pallas_call(
        functools.partial(_merge_ffn_kernel, rows=rows, tiles_per_seg=tps),
        grid=(ntok // rows,),
        in_specs=[pl.BlockSpec((NBLK, pitch, LANES), lambda i: (0, i // tps, 0)), tok, tok, tok]
                 + [_layer_spec(w.shape, layer) for w in (w_glu, w_o, npost, npre, w1, w2, npostff)],
        out_specs=tok,
        out_shape=jax.ShapeDtypeStruct((ntok, D_MODEL), _F32),
        compiler_params=pltpu.CompilerParams(dimension_semantics=("arbitrary",),
                                             vmem_limit_bytes=FUSED_VMEM_LIMIT),
        name="merge_ffn",
    )(y_pad, ga, gb, x, w_glu, w_o, npost, npre, w1, w2, npostff)


def _trunk(x, p, s5m):
    nseq, seqlen, d = x.shape
    seg = seqlen // NSEG
    assert d == D_MODEL and seg % GMLP_CHUNK == 0 and seqlen % TOK_ROWS == 0
    x = x.reshape(nseq * seqlen, d)
    t_mat, w_si, w_so, pw = s5m
    for l in range(p["w_in"].shape[0]):
        ga, u_pad, gb = _inproj(x, p["norm_pre_mix"], p["w_in"], p["norm_v"], p["w_s"], p["bias"],
                                p["w_out_a"], l, seg)
        y_pad = _s5_core(u_pad, t_mat, w_si, w_so, pw, p["d_skip"], l, nseq, seg)
        x = _merge_ffn(y_pad, ga, gb, x, p["w_glu"], p["w_o"], p["norm_post_mix"],
                       p["norm_pre_ff"], p["w_ff1"], p["w_ff2"], p["norm_post_ff"], l, seg)
    return x.reshape(nseq, seqlen, d)


def kernel(x_prompt, x_sample, norm_pre_mix, w_in, norm_v, w_s, b_s, w_out_a, lam_re, lam_im,
           log_dt, b_re, b_im, c_re, c_im, d_skip, w_glu, w_o, norm_post_mix, norm_pre_ff,
           w_ff1, w_ff2, norm_post_ff):
    depth = w_in.shape[0]
    bf = lambda w: w.astype(_BF16)
    row = lambda v: v.reshape(depth, 1, D_MODEL)
    p = {
        "norm_pre_mix": row(norm_pre_mix), "w_in": bf(w_in), "norm_v": row(norm_v), "w_s": bf(w_s),
        "bias": jnp.repeat(jnp.swapaxes(b_s, 1, 2), LANES, axis=2),
        "w_out_a": bf(w_out_a), "d_skip": d_skip.reshape(depth, NBLK, 1, LANES),
        "w_glu": bf(w_glu), "w_o": bf(w_o), "norm_post_mix": row(norm_post_mix),
        "norm_pre_ff": row(norm_pre_ff), "w_ff1": bf(w_ff1), "w_ff2": bf(w_ff2),
        "norm_post_ff": row(norm_post_ff),
    }
    outs = []
    mats = {}
    for x in (x_prompt, x_sample):
        nk = x.shape[1] // (NSEG * CHUNK_L)
        if nk not in mats:
            mats[nk] = _s5_matrices(lam_re, lam_im, log_dt, b_re, b_im, c_re, c_im, nk)
        outs.append(_trunk(x, p, mats[nk]))
    return tuple(outs)
```

```python
import functools

import jax
import jax.numpy as jnp
from jax import lax
from jax.experimental import pallas as pl
from jax.experimental.pallas import tpu as pltpu

D_MODEL = 1024
GMLP_CHUNK = 128
A_GROUPS = 8
GROUPS = 64
GROUP_DIM = 16
STATE = 64
D_FF = 4 * D_MODEL
EPS = 1e-6

LANES = 128
SUBLANES = 8
CHUNK_L = 16
NSEG = SUBLANES
NBLK = D_MODEL // LANES
PAIR_LANES = 2 * GROUP_DIM
PAIRS_PER_BLK = LANES // PAIR_LANES
NPAIR = GROUPS // 2
PAIR_K = CHUNK_L * PAIR_LANES
PAIR_STATE = 2 * STATE
BLK_STATE = PAIRS_PER_BLK * PAIR_STATE
SEG_PAD = SUBLANES
ROW_BLK = 2 * SUBLANES
KPB = ROW_BLK // SUBLANES
SCAN_UNROLL = True
PREP_PAIRS = 2
TOK_ROWS = 512
INPROJ_ROWS = 1024
PART_ROWS = 256
VMEM_LIMIT = 56 * 1024 * 1024
FUSED_VMEM_LIMIT = 60 * 1024 * 1024

_F32 = jnp.float32
_BF16 = jnp.bfloat16


def _dot(a, b):
    return jnp.dot(a, b, preferred_element_type=_F32)


def _rms(x, g):
    return x * lax.rsqrt(jnp.mean(x * x, axis=-1, keepdims=True) + EPS) * g


def _sigmoid(x):
    return 1.0 / (1.0 + jnp.exp(-x))


def _gelu(x):
    return jax.nn.gelu(x, approximate=True)


def _layer_spec(shape, layer):
    nd = len(shape)
    return pl.BlockSpec((1,) + tuple(shape[1:]), lambda *_: (layer,) + (0,) * (nd - 1),
                        pipeline_mode=pl.Buffered(1))


def _params(n_axes):
    return pltpu.CompilerParams(
        dimension_semantics=("arbitrary",) * n_axes, vmem_limit_bytes=VMEM_LIMIT)


def _cmul(ar, ai, br, bi):
    return ar * br - ai * bi, ar * bi + ai * br


def _wprep_pair(lre_ref, lim_ref, ldt_ref, bre_ref, bim_ref, cre_ref, cim_ref,
                  t_ref, wsi_ref, wso_ref, pw_ref,
                  xr_ref, xi_ref, yr_ref, yni_ref, eo_ref, *, nk, q):
    L = CHUNK_L
    jpos = lax.rem(pl.program_id(1) * PREP_PAIRS + q, PAIRS_PER_BLK)
    lane = lax.broadcasted_iota(jnp.int32, (GROUP_DIM, LANES), 1)
    lane1 = lax.broadcasted_iota(jnp.int32, (1, LANES), 1)
    for d in range(2):
        a_pair = []
        for g2 in range(2):
            lre, lim = lre_ref[0, d, 2 * q + g2], lim_ref[0, d, 2 * q + g2]
            dt = jnp.exp(ldt_ref[0, d, 2 * q + g2])
            mag = jnp.exp(lre * dt)
            ar, ai = mag * jnp.cos(lim * dt), mag * jnp.sin(lim * dt)
            a_pair.append((ar, ai))
            den = lre * lre + lim * lim
            qr = ((ar - 1.0) * lre + ai * lim) / den
            qi = (ai * lre - (ar - 1.0) * lim) / den
            n2 = ar * ar + ai * ai
            ir, ii = ar / n2, -ai / n2
            keep = (lane >= g2 * STATE) & (lane < (g2 + 1) * STATE)
            pr, pi = _cmul(qr, qi, jnp.where(keep, bre_ref[0, d, 2 * q + g2], 0.0),
                           jnp.where(keep, bim_ref[0, d, 2 * q + g2], 0.0))
            nr, ni = pr, pi
            fr = jnp.where(keep, cre_ref[0, d, 2 * q + g2], 0.0)
            fi = jnp.where(keep, cim_ref[0, d, 2 * q + g2], 0.0)
            gr, gi = fr, fi

            def rows(k):
                gran = lax.rem(jpos + (k % PAIRS_PER_BLK), PAIRS_PER_BLK)
                start = (k // PAIRS_PER_BLK) * LANES + gran * PAIR_LANES + g2 * GROUP_DIM
                return pl.ds(pl.multiple_of(start, GROUP_DIM), GROUP_DIM)

            for k in range(L + 1):
                if k < L:
                    if d == 0:
                        xr_ref[2 * q, rows(k), :] = nr
                        xi_ref[2 * q, rows(k), :] = ni
                        yr_ref[2 * q, rows(k), :] = fr
                        yni_ref[2 * q, rows(k), :] = -fi
                        wsi_ref[0, q, rows(L - 1 - k), 0:LANES] = pr.astype(_BF16)
                        wsi_ref[0, q, rows(L - 1 - k), LANES:2 * LANES] = pi.astype(_BF16)
                    else:
                        xr_ref[2 * q + 1, rows(k), :] = pr
                        xi_ref[2 * q + 1, rows(k), :] = pi
                        yr_ref[2 * q + 1, rows(k), :] = gr
                        yni_ref[2 * q + 1, rows(k), :] = -gi
                        wsi_ref[0, q, rows(k), 2 * LANES:3 * LANES] = pr.astype(_BF16)
                        wsi_ref[0, q, rows(k), 3 * LANES:4 * LANES] = pi.astype(_BF16)
                if k >= 1:
                    t = k - 1 if d == 0 else L - k
                    eo_ref[4 * q + 2 * d, rows(t), :] = fr
                    eo_ref[4 * q + 2 * d + 1, rows(t), :] = -fi
                pr, pi = _cmul(ar, ai, pr, pi)
                nr, ni = _cmul(ir, ii, nr, ni)
                fr, fi = _cmul(ar, ai, fr, fi)
                gr, gi = _cmul(ir, ii, gr, gi)

        a_r = jnp.where(lane1 < STATE, a_pair[0][0], a_pair[1][0])
        a_i = jnp.where(lane1 < STATE, a_pair[0][1], a_pair[1][1])
        alr, ali = a_r, a_i
        for _ in range(L - 1):
            alr, ali = _cmul(alr, ali, a_r, a_i)
        one, zero = jnp.ones((1, LANES), _F32), jnp.zeros((1, LANES), _F32)
        pw_ref[0, q, d, 0, 0:1, :] = one
        pw_ref[0, q, d, 1, 0:1, :] = zero

        def power(k, carry):
            wr, wi = _cmul(carry[0], carry[1], alr, ali)
            pw_ref[0, q, d, 0, pl.ds(k + 1, 1), :] = wr
            pw_ref[0, q, d, 1, pl.ds(k + 1, 1), :] = wi
            return wr, wi

        lax.fori_loop(0, nk, power, (one, zero), unroll=True)

    def split(v):
        hi = v.astype(_BF16)
        return hi, (v - hi.astype(_F32)).astype(_BF16)

    def lag_matrix(d):
        xh, xl = split(jnp.concatenate([xr_ref[2 * q + d], xi_ref[2 * q + d]], axis=1))
        yh, yl = split(jnp.concatenate([yr_ref[2 * q + d], yni_ref[2 * q + d]], axis=1))
        return lax.dot_general(jnp.concatenate([xh, xh, xl], axis=1), jnp.concatenate([yh, yl, yh], axis=1),
                               (((1,), (1,)), ((), ())), preferred_element_type=_F32)

    def time_of(axis):
        pos = lax.broadcasted_iota(jnp.int32, (PAIR_K, PAIR_K), axis)
        gran = (pos >> (PAIR_LANES.bit_length() - 1)) & (PAIRS_PER_BLK - 1)
        return (pos >> (LANES.bit_length() - 1)) * PAIRS_PER_BLK + ((gran - jpos) & (PAIRS_PER_BLK - 1))

    tau, t = time_of(0), time_of(1)
    t_ref[0, q] = (jnp.where(t >= tau, lag_matrix(0), 0.0)
                   + jnp.where(tau >= t, lag_matrix(1), 0.0)).astype(_BF16)
    for j in range(4):
        wso_ref[0, q, j * LANES:(j + 1) * LANES, :] = eo_ref[4 * q + j].T.astype(_BF16)


def _wprep_kernel(*refs, nk):
    for q in range(PREP_PAIRS):
        _wprep_pair(*refs, nk=nk, q=q)


def _s5_matrices(lam_re, lam_im, log_dt, b_re, b_im, c_re, c_im, nk):
    depth = lam_re.shape[0]
    dup = lambda v: jnp.concatenate([v, v], axis=-1)
    row = lambda v: dup(v)[:, :, :, None, :]
    ldt = jnp.broadcast_to(log_dt[..., None, None], (depth, 2, GROUPS, 1, LANES))
    bt = lambda v: dup(jnp.swapaxes(v, -1, -2))
    vec = pl.BlockSpec((1, 2, 2 * PREP_PAIRS, 1, LANES), lambda l, q: (l, 0, q, 0, 0))
    mat = pl.BlockSpec((1, 2, 2 * PREP_PAIRS, GROUP_DIM, LANES), lambda l, q: (l, 0, q, 0, 0))
    sq = pl.BlockSpec((1, PREP_PAIRS, PAIR_K, PAIR_K), lambda l, q: (l, q, 0, 0))
    sq_shape = jax.ShapeDtypeStruct((depth, NPAIR, PAIR_K, PAIR_K), _BF16)
    return pl.pallas_call(
        functools.partial(_wprep_kernel, nk=nk),
        grid=(depth, NPAIR // PREP_PAIRS),
        in_specs=[vec, vec, vec, mat, mat, mat, mat],
        out_specs=[sq, sq, sq,
                   pl.BlockSpec((1, PREP_PAIRS, 2, 2, nk + 1, LANES), lambda l, q: (l, q, 0, 0, 0, 0))],
        out_shape=[sq_shape, sq_shape, sq_shape,
                   jax.ShapeDtypeStruct((depth, NPAIR, 2, 2, nk + 1, LANES), _F32)],
        scratch_shapes=[pltpu.VMEM((2 * PREP_PAIRS, PAIR_K, LANES), _F32)] * 4
                       + [pltpu.VMEM((4 * PREP_PAIRS, PAIR_K, LANES), _F32)],
        compiler_params=_params(2),
        name="s5_weight_prep",
    )(row(lam_re), row(lam_im), ldt, bt(b_re), bt(b_im), dup(c_re), dup(c_im))


def _inproj_kernel(x_ref, npm_ref, win_ref, nv_ref, ws_ref, bs_ref, woa_ref,
                   ga_ref, u_ref, gb_ref, mix_ref, *, rows, tiles_per_seg, seg):
    d = D_MODEL
    prow = min(PART_ROWS, rows)
    nchunk = prow // GMLP_CHUNK
    parts = [slice(q * prow, (q + 1) * prow) for q in range(rows // prow)]
    h = [_rms(x_ref[p, :], npm_ref[0]).astype(_BF16) for p in parts]
    vn = [_rms(_gelu(_dot(hq, win_ref[0, :, d:2 * d])), nv_ref[0]).astype(_BF16) for hq in h]
    r0 = pl.multiple_of((pl.program_id(0) % tiles_per_seg) * rows, rows)
    for p, hq in zip(parts, h):
        ub = _dot(hq, win_ref[0, :, 2 * d:3 * d])
        for b in range(NBLK):
            u_ref[b, pl.ds(r0 + p.start, prow), :] = ub[:, b * LANES:(b + 1) * LANES]
        gb_ref[p, :] = _sigmoid(_dot(hq, win_ref[0, :, 4 * d:5 * d])).astype(_BF16)
    u_ref[:, seg:seg + SEG_PAD, :] = jnp.zeros((NBLK, SEG_PAD, LANES), _F32)
    ua = [_gelu(_dot(hq, win_ref[0, :, 0:d])) for hq in h]
    for p, vq in zip(parts, vn):
        for g in range(A_GROUPS):
            cs = slice(g * LANES, (g + 1) * LANES)
            v_all = jnp.concatenate([vq[n * GMLP_CHUNK:(n + 1) * GMLP_CHUNK, cs] for n in range(nchunk)], axis=1)
            mixed = _dot(ws_ref[0, g], v_all)
            for n in range(nchunk):
                mix_ref[p.start + n * GMLP_CHUNK:p.start + (n + 1) * GMLP_CHUNK, cs] = (
                    mixed[:, n * LANES:(n + 1) * LANES] + bs_ref[0, :, cs])
    for p, hq, uq in zip(parts, h, ua):
        a = _dot((uq * mix_ref[p, :]).astype(_BF16), woa_ref[0])
        ga_ref[p, :] = (_sigmoid(_dot(hq, win_ref[0, :, 3 * d:4 * d])) * a).astype(_BF16)


def _inproj(x, npm, w_in, nv, w_s, bias, w_out_a, layer, seg):
    ntok = x.shape[0]
    rows = min(INPROJ_ROWS, seg)
    tps = seg // rows
    pitch = seg + SEG_PAD
    tok = pl.BlockSpec((rows, D_MODEL), lambda i: (i, 0))
    return pl.pallas_call(
        functools.partial(_inproj_kernel, rows=rows, tiles_per_seg=tps, seg=seg),
        grid=(ntok // rows,),
        in_specs=[tok] + [_layer_spec(w.shape, layer) for w in (npm, w_in, nv, w_s, bias, w_out_a)],
        out_specs=[tok, pl.BlockSpec((NBLK, pitch, LANES), lambda i: (0, i // tps, 0)), tok],
        out_shape=[jax.ShapeDtypeStruct((ntok, D_MODEL), _BF16),
                   jax.ShapeDtypeStruct((NBLK, ntok // seg * pitch, LANES), _F32),
                   jax.ShapeDtypeStruct((ntok, D_MODEL), _BF16)],
        scratch_shapes=[pltpu.VMEM((rows, D_MODEL), _F32)],
        compiler_params=_params(1),
        name="inproj_gmlp",
    )(x, npm, w_in, nv, w_s, bias, w_out_a)


def _s5_kernel(u_ref, t_ref, wi_ref, wo_ref, pw_ref, skip_ref, y_ref,
               lhs_ref, z_ref, s_ref, yp_ref, *, nk, pitch):
    ns = BLK_STATE
    npair = PAIRS_PER_BLK
    tgroups = CHUNK_L // npair

    def seg_rows(k, tau):
        return pl.ds(k * CHUNK_L + tau, NSEG, stride=pitch)

    def power_row(d, c, row):
        return jnp.concatenate([pw_ref[0, j, d, c, row, :] for j in range(npair)], axis=1)

    def pick(gran, pieces, base):
        out = pieces[(npair - 1 - base) % npair]
        for g in range(npair - 2, -1, -1):
            out = jnp.where(gran == g, pieces[(g - base) % npair], out)
        return out

    def gather(tg):
        def body(i, _):
            r = pl.ds(pl.multiple_of(i * ROW_BLK, ROW_BLK), ROW_BLK)
            gran = lax.broadcasted_iota(jnp.int32, (ROW_BLK, LANES), 1) >> (PAIR_LANES.bit_length() - 1)
            w = []
            for tl in range(npair):
                v = jnp.concatenate([u_ref[seg_rows(i * KPB + kk, tg * npair + tl), :] for kk in range(KPB)],
                                    axis=0)
                w.append(pltpu.roll(v, tl * PAIR_LANES, 1) if tl else v)
            for j in range(npair):
                lhs_ref[j, r, tg * LANES:(tg + 1) * LANES] = pick(gran, w, j).astype(_BF16)
            return 0

        lax.fori_loop(0, nk // KPB, body, 0, unroll=True)

    for tg in range(tgroups):
        gather(tg)

    seg_id = lax.broadcasted_iota(jnp.int32, (NSEG, ns), 0)
    for d in range(2):
        for j in range(PAIRS_PER_BLK):
            zz = _dot(lhs_ref[j], wi_ref[0, j, :, d * 2 * LANES:(d + 1) * 2 * LANES])
            z_ref[d, :, j * LANES:(j + 1) * LANES] = zz[:, 0:LANES]
            z_ref[d, :, ns + j * LANES:ns + (j + 1) * LANES] = zz[:, LANES:2 * LANES]
    for j in range(PAIRS_PER_BLK):
        yp_ref[j] = _dot(lhs_ref[j], t_ref[0, j])

    for d in range(2):
        backward = d == 1
        al_r = jnp.broadcast_to(power_row(d, 0, slice(1, 2)), (NSEG, ns))
        al_i = jnp.broadcast_to(power_row(d, 1, slice(1, 2)), (NSEG, ns))

        def local_scan(j, carry):
            k = nk - 1 - j if backward else j
            r = pl.ds(pl.multiple_of(k * SUBLANES, SUBLANES), SUBLANES)
            sr, si = carry
            zr, zi = z_ref[d, r, 0:ns], z_ref[d, r, ns:2 * ns]
            z_ref[d, r, 0:ns] = sr
            z_ref[d, r, ns:2 * ns] = si
            pr, pi = _cmul(al_r, al_i, sr, si)
            return pr + zr, pi + zi

        zero = jnp.zeros((NSEG, ns), _F32)
        tot_r, tot_i = lax.fori_loop(0, nk, local_scan, (zero, zero), unroll=SCAN_UNROLL)

        as_r = jnp.broadcast_to(power_row(d, 0, slice(nk, nk + 1)), (NSEG, ns))
        as_i = jnp.broadcast_to(power_row(d, 1, slice(nk, nk + 1)), (NSEG, ns))
        edge = NSEG - 1 if backward else 0
        shift = NSEG - 1 if backward else 1
        cr, ci = zero, zero
        for _ in range(NSEG - 1):
            pr, pi = _cmul(as_r, as_i, cr, ci)
            cr = jnp.where(seg_id == edge, 0.0, pltpu.roll(pr + tot_r, shift, 0))
            ci = jnp.where(seg_id == edge, 0.0, pltpu.roll(pi + tot_i, shift, 0))

        def add_carry(i, _):
            r = pl.ds(pl.multiple_of(i * ROW_BLK, ROW_BLK), ROW_BLK)
            sr, si = [], []
            for kk in range(KPB):
                k = i * KPB + kk
                e = nk - 1 - k if backward else k
                pr, pi = _cmul(power_row(d, 0, pl.ds(e, 1)), power_row(d, 1, pl.ds(e, 1)), cr, ci)
                rk = pl.ds(pl.multiple_of(k * SUBLANES, SUBLANES), SUBLANES)
                sr.append(z_ref[d, rk, 0:ns] + pr)
                si.append(z_ref[d, rk, ns:2 * ns] + pi)
            s_ref[r, d * 2 * ns:d * 2 * ns + ns] = jnp.concatenate(sr, axis=0).astype(_BF16)
            s_ref[r, d * 2 * ns + ns:(d + 1) * 2 * ns] = jnp.concatenate(si, axis=0).astype(_BF16)
            return 0

        lax.fori_loop(0, nk // KPB, add_carry, 0, unroll=SCAN_UNROLL)

    for j in range(PAIRS_PER_BLK):
        sp = jnp.concatenate([s_ref[:, q * ns + j * LANES:q * ns + (j + 1) * LANES] for q in range(4)],
                             axis=1)
        yp_ref[j] = yp_ref[j] + _dot(sp, wo_ref[0, j])

    skip = skip_ref[0, 0]

    def scatter(tg):
        def body(k, _):
            r = pl.ds(pl.multiple_of(k * SUBLANES, SUBLANES), SUBLANES)
            gran = lax.broadcasted_iota(jnp.int32, (SUBLANES, LANES), 1) >> (PAIR_LANES.bit_length() - 1)
            yb = [yp_ref[j, r, tg * LANES:(tg + 1) * LANES] for j in range(npair)]
            for tl in range(npair):
                v = pick(gran, yb, tl)
                v = pltpu.roll(v, LANES - tl * PAIR_LANES, 1) if tl else v
                rows = seg_rows(k, tg * npair + tl)
                y_ref[rows, :] = v + skip * u_ref[rows, :]
            return 0

        lax.fori_loop(0, nk, body, 0, unroll=True)

    for tg in range(tgroups):
        scatter(tg)
    seg = nk * CHUNK_L
    for s in range(NSEG):
        y_ref[s * pitch + seg:(s + 1) * pitch, :] = jnp.zeros((SEG_PAD, LANES), _F32)


def _s5_core(u_pad, t_mat, w_si, w_so, pw, skip, layer, nseq, seg):
    nk = seg // CHUNK_L
    pitch = seg + SEG_PAD
    m = nk * NSEG
    seq_blk = pl.BlockSpec((None, NSEG * pitch, LANES), lambda b, s: (b, s, 0))
    sq = pl.BlockSpec((1, PAIRS_PER_BLK, PAIR_K, PAIR_K), lambda b, s: (layer, b, 0, 0))
    return pl.pallas_call(
        functools.partial(_s5_kernel, nk=nk, pitch=pitch),
        grid=(NBLK, nseq),
        in_specs=[seq_blk, sq, sq, sq,
                  pl.BlockSpec((1, PAIRS_PER_BLK, 2, 2, nk + 1, LANES), lambda b, s: (layer, b, 0, 0, 0, 0)),
                  pl.BlockSpec((1, 1, 1, LANES), lambda b, s: (layer, b, 0, 0))],
        out_specs=seq_blk,
        out_shape=jax.ShapeDtypeStruct(u_pad.shape, _F32),
        scratch_shapes=[pltpu.VMEM((PAIRS_PER_BLK, m, PAIR_K), _BF16),
                        pltpu.VMEM((2, m, 2 * BLK_STATE), _F32),
                        pltpu.VMEM((m, 4 * BLK_STATE), _BF16),
                        pltpu.VMEM((PAIRS_PER_BLK, m, PAIR_K), _F32)],
        compiler_params=_params(2),
        name="s5_core",
    )(u_pad, t_mat, w_si, w_so, pw, skip)


def _merge_ffn_kernel(y_ref, ga_ref, gb_ref, x_ref, wglu_ref, wo_ref, npost_ref,
                      npre_ref, w1_ref, w2_ref, npostff_ref, o_ref, *, rows, tiles_per_seg):
    d = D_MODEL
    r0 = pl.multiple_of((pl.program_id(0) % tiles_per_seg) * rows, rows)
    prow = min(PART_ROWS, rows)
    parts = [slice(q * prow, (q + 1) * prow) for q in range(rows // prow)]
    def y_rows(p):
        return jnp.concatenate([y_ref[b, pl.ds(r0 + p.start, p.stop - p.start), :] for b in range(NBLK)], axis=1)

    z = [_gelu(y_rows(p)).astype(_BF16) for p in parts]
    g1 = [_dot(zq, wglu_ref[0, :, 0:d]) for zq in z]
    g2 = [_dot(zq, wglu_ref[0, :, d:2 * d]) for zq in z]
    m = [(ga_ref[p, :].astype(_F32) + gb_ref[p, :].astype(_F32) * (a * _sigmoid(b))).astype(_BF16)
         for p, a, b in zip(parts, g1, g2)]
    x1 = [x_ref[p, :] + _rms(_dot(mq, wo_ref[0]), npost_ref[0]) for p, mq in zip(parts, m)]
    h = [_rms(xq, npre_ref[0]).astype(_BF16) for xq in x1]
    f = [jnp.zeros(xq.shape, _F32) for xq in x1]
    for j in range(D_FF // D_MODEL):
        cs = slice(j * D_MODEL, (j + 1) * D_MODEL)
        for q, hq in enumerate(h):
            r = jnp.maximum(_dot(hq, w1_ref[0, :, cs]), 0.0)
            f[q] = f[q] + _dot((r * r).astype(_BF16), w2_ref[0, cs, :])
    for p, xq, fq in zip(parts, x1, f):
        o_ref[p, :] = xq + _rms(fq, npostff_ref[0])


def _merge_ffn(y_pad, ga, gb, x, w_glu, w_o, npost, npre, w1, w2, npostff, layer, seg):
    ntok = x.shape[0]
    rows = min(TOK_ROWS, seg)
    tps = seg // rows
    pitch = seg + SEG_PAD
    tok = pl.BlockSpec((rows, D_MODEL), lambda i: (i, 0))
    return pl.pallas_call(
        functools.partial(_merge_ffn_kernel, rows=rows, tiles_per_seg=tps),
        grid=(ntok // rows,),
        in_specs=[pl.BlockSpec((NBLK, pitch, LANES), lambda i: (0, i // tps, 0)), tok, tok, tok]
                 + [_layer_spec(w.shape, layer) for w in (w_glu, w_o, npost, npre, w1, w2, npostff)],
        out_specs=tok,
        out_shape=jax.ShapeDtypeStruct((ntok, D_MODEL), _F32),
        compiler_params=pltpu.CompilerParams(dimension_semantics=("arbitrary",),
                                             vmem_limit_bytes=FUSED_VMEM_LIMIT),
        name="merge_ffn",
    )(y_pad, ga, gb, x, w_glu, w_o, npost, npre, w1, w2, npostff)


def _trunk(x, p, s5m):
    nseq, seqlen, d = x.shape
    seg = seqlen // NSEG
    assert d == D_MODEL and seg % GMLP_CHUNK == 0 and seqlen % TOK_ROWS == 0
    x = x.reshape(nseq * seqlen, d)
    t_mat, w_si, w_so, pw = s5m
    for l in range(p["w_in"].shape[0]):
        ga, u_pad, gb = _inproj(x, p["norm_pre_mix"], p["w_in"], p["norm_v"], p["w_s"], p["bias"],
                                p["w_out_a"], l, seg)
        y_pad = _s5_core(u_pad, t_mat, w_si, w_so, pw, p["d_skip"], l, nseq, seg)
        x = _merge_ffn(y_pad, ga, gb, x, p["w_glu"], p["w_o"], p["norm_post_mix"],
                       p["norm_pre_ff"], p["w_ff1"], p["w_ff2"], p["norm_post_ff"], l, seg)
    return x.reshape(nseq, seqlen, d)


def kernel(x_prompt, x_sample, norm_pre_mix, w_in, norm_v, w_s, b_s, w_out_a, lam_re, lam_im,
           log_dt, b_re, b_im, c_re, c_im, d_skip, w_glu, w_o, norm_post_mix, norm_pre_ff,
           w_ff1, w_ff2, norm_post_ff):
    depth = w_in.shape[0]
    bf = lambda w: w.astype(_BF16)
    row = lambda v: v.reshape(depth, 1, D_MODEL)
    p = {
        "norm_pre_mix": row(norm_pre_mix), "w_in": bf(w_in), "norm_v": row(norm_v), "w_s": bf(w_s),
        "bias": jnp.repeat(jnp.swapaxes(b_s, 1, 2), LANES, axis=2),
        "w_out_a": bf(w_out_a), "d_skip": d_skip.reshape(depth, NBLK, 1, LANES),
        "w_glu": bf(w_glu), "w_o": bf(w_o), "norm_post_mix": row(norm_post_mix),
        "norm_pre_ff": row(norm_pre_ff), "w_ff1": bf(w_ff1), "w_ff2": bf(w_ff2),
        "norm_post_ff": row(norm_post_ff),
    }
    outs = []
    mats = {}
    for x in (x_prompt, x_sample):
        nk = x.shape[1] // (NSEG * CHUNK_L)
        if nk not in mats:
            mats[nk] = _s5_matrices(lam_re, lam_im, log_dt, b_re, b_im, c_re, c_im, nk)
        outs.append(_trunk(x, p, mats[nk]))
    return tuple(outs)
```

```python
import functools

import jax
import jax.numpy as jnp
from jax import lax
from jax.experimental import pallas as pl
from jax.experimental.pallas import tpu as pltpu

D_MODEL = 1024
GMLP_CHUNK = 128
A_GROUPS = 8
GROUPS = 64
GROUP_DIM = 16
STATE = 64
D_FF = 4 * D_MODEL
EPS = 1e-6

LANES = 128
SUBLANES = 8
CHUNK_L = 16
NSEG = SUBLANES
NBLK = D_MODEL // LANES
PAIR_LANES = 2 * GROUP_DIM
PAIRS_PER_BLK = LANES // PAIR_LANES
NPAIR = GROUPS // 2
PAIR_K = CHUNK_L * PAIR_LANES
PAIR_STATE = 2 * STATE
BLK_STATE = PAIRS_PER_BLK * PAIR_STATE
SEG_PAD = SUBLANES
ROW_BLK = 2 * SUBLANES
KPB = ROW_BLK // SUBLANES
SCAN_UNROLL = True
ROW_SPLITS = 2
PREP_PAIRS = 2
TOK_ROWS = 512
INPROJ_ROWS = 1024
PART_ROWS = 256
VMEM_LIMIT = 56 * 1024 * 1024
FUSED_VMEM_LIMIT = 60 * 1024 * 1024

_F32 = jnp.float32
_BF16 = jnp.bfloat16


def _dot(a, b):
    return jnp.dot(a, b, preferred_element_type=_F32)


def _rms(x, g):
    return x * lax.rsqrt(jnp.mean(x * x, axis=-1, keepdims=True) + EPS) * g


def _sigmoid(x):
    return 1.0 / (1.0 + jnp.exp(-x))


def _gelu(x):
    return jax.nn.gelu(x, approximate=True)


def _layer_spec(shape, layer):
    nd = len(shape)
    return pl.BlockSpec((1,) + tuple(shape[1:]), lambda *_: (layer,) + (0,) * (nd - 1),
                        pipeline_mode=pl.Buffered(1))


def _params(n_axes):
    return pltpu.CompilerParams(
        dimension_semantics=("arbitrary",) * n_axes, vmem_limit_bytes=VMEM_LIMIT)


def _cmul(ar, ai, br, bi):
    return ar * br - ai * bi, ar * bi + ai * br


def _wprep_pair(lre_ref, lim_ref, ldt_ref, bre_ref, bim_ref, cre_ref, cim_ref,
                  t_ref, wsi_ref, wso_ref, pw_ref,
                  xr_ref, xi_ref, yr_ref, yni_ref, eo_ref, *, nk, q):
    L = CHUNK_L
    jpos = lax.rem(pl.program_id(1) * PREP_PAIRS + q, PAIRS_PER_BLK)
    lane = lax.broadcasted_iota(jnp.int32, (GROUP_DIM, LANES), 1)
    lane1 = lax.broadcasted_iota(jnp.int32, (1, LANES), 1)
    for d in range(2):
        a_pair = []
        for g2 in range(2):
            lre, lim = lre_ref[0, d, 2 * q + g2], lim_ref[0, d, 2 * q + g2]
            dt = jnp.exp(ldt_ref[0, d, 2 * q + g2])
            mag = jnp.exp(lre * dt)
            ar, ai = mag * jnp.cos(lim * dt), mag * jnp.sin(lim * dt)
            a_pair.append((ar, ai))
            den = lre * lre + lim * lim
            qr = ((ar - 1.0) * lre + ai * lim) / den
            qi = (ai * lre - (ar - 1.0) * lim) / den
            n2 = ar * ar + ai * ai
            ir, ii = ar / n2, -ai / n2
            keep = (lane >= g2 * STATE) & (lane < (g2 + 1) * STATE)
            pr, pi = _cmul(qr, qi, jnp.where(keep, bre_ref[0, d, 2 * q + g2], 0.0),
                           jnp.where(keep, bim_ref[0, d, 2 * q + g2], 0.0))
            nr, ni = pr, pi
            fr = jnp.where(keep, cre_ref[0, d, 2 * q + g2], 0.0)
            fi = jnp.where(keep, cim_ref[0, d, 2 * q + g2], 0.0)
            gr, gi = fr, fi

            def rows(k):
                gran = lax.rem(jpos + (k % PAIRS_PER_BLK), PAIRS_PER_BLK)
                start = (k // PAIRS_PER_BLK) * LANES + gran * PAIR_LANES + g2 * GROUP_DIM
                return pl.ds(pl.multiple_of(start, GROUP_DIM), GROUP_DIM)

            for k in range(L + 1):
                if k < L:
                    if d == 0:
                        xr_ref[2 * q, rows(k), :] = nr
                        xi_ref[2 * q, rows(k), :] = ni
                        yr_ref[2 * q, rows(k), :] = fr
                        yni_ref[2 * q, rows(k), :] = -fi
                        wsi_ref[0, q, rows(L - 1 - k), 0:LANES] = pr.astype(_BF16)
                        wsi_ref[0, q, rows(L - 1 - k), LANES:2 * LANES] = pi.astype(_BF16)
                    else:
                        xr_ref[2 * q + 1, rows(k), :] = pr
                        xi_ref[2 * q + 1, rows(k), :] = pi
                        yr_ref[2 * q + 1, rows(k), :] = gr
                        yni_ref[2 * q + 1, rows(k), :] = -gi
                        wsi_ref[0, q, rows(k), 2 * LANES:3 * LANES] = pr.astype(_BF16)
                        wsi_ref[0, q, rows(k), 3 * LANES:4 * LANES] = pi.astype(_BF16)
                if k >= 1:
                    t = k - 1 if d == 0 else L - k
                    eo_ref[4 * q + 2 * d, rows(t), :] = fr
                    eo_ref[4 * q + 2 * d + 1, rows(t), :] = -fi
                pr, pi = _cmul(ar, ai, pr, pi)
                nr, ni = _cmul(ir, ii, nr, ni)
                fr, fi = _cmul(ar, ai, fr, fi)
                gr, gi = _cmul(ir, ii, gr, gi)

        a_r = jnp.where(lane1 < STATE, a_pair[0][0], a_pair[1][0])
        a_i = jnp.where(lane1 < STATE, a_pair[0][1], a_pair[1][1])
        alr, ali = a_r, a_i
        for _ in range(L - 1):
            alr, ali = _cmul(alr, ali, a_r, a_i)
        one, zero = jnp.ones((1, LANES), _F32), jnp.zeros((1, LANES), _F32)
        pw_ref[0, q, d, 0, 0:1, :] = one
        pw_ref[0, q, d, 1, 0:1, :] = zero

        def power(k, carry):
            wr, wi = _cmul(carry[0], carry[1], alr, ali)
            pw_ref[0, q, d, 0, pl.ds(k + 1, 1), :] = wr
            pw_ref[0, q, d, 1, pl.ds(k + 1, 1), :] = wi
            return wr, wi

        lax.fori_loop(0, nk, power, (one, zero), unroll=True)

    def split(v):
        hi = v.astype(_BF16)
        return hi, (v - hi.astype(_F32)).astype(_BF16)

    def lag_matrix(d):
        xh, xl = split(jnp.concatenate([xr_ref[2 * q + d], xi_ref[2 * q + d]], axis=1))
        yh, yl = split(jnp.concatenate([yr_ref[2 * q + d], yni_ref[2 * q + d]], axis=1))
        return lax.dot_general(jnp.concatenate([xh, xh, xl], axis=1), jnp.concatenate([yh, yl, yh], axis=1),
                               (((1,), (1,)), ((), ())), preferred_element_type=_F32)

    def time_of(axis):
        pos = lax.broadcasted_iota(jnp.int32, (PAIR_K, PAIR_K), axis)
        gran = (pos >> (PAIR_LANES.bit_length() - 1)) & (PAIRS_PER_BLK - 1)
        return (pos >> (LANES.bit_length() - 1)) * PAIRS_PER_BLK + ((gran - jpos) & (PAIRS_PER_BLK - 1))

    tau, t = time_of(0), time_of(1)
    t_ref[0, q] = (jnp.where(t >= tau, lag_matrix(0), 0.0)
                   + jnp.where(tau >= t, lag_matrix(1), 0.0)).astype(_BF16)
    for j in range(4):
        wso_ref[0, q, j * LANES:(j + 1) * LANES, :] = eo_ref[4 * q + j].T.astype(_BF16)


def _wprep_kernel(*refs, nk):
    for q in range(PREP_PAIRS):
        _wprep_pair(*refs, nk=nk, q=q)


def _s5_matrices(lam_re, lam_im, log_dt, b_re, b_im, c_re, c_im, nk):
    depth = lam_re.shape[0]
    dup = lambda v: jnp.concatenate([v, v], axis=-1)
    row = lambda v: dup(v)[:, :, :, None, :]
    ldt = jnp.broadcast_to(log_dt[..., None, None], (depth, 2, GROUPS, 1, LANES))
    bt = lambda v: dup(jnp.swapaxes(v, -1, -2))
    vec = pl.BlockSpec((1, 2, 2 * PREP_PAIRS, 1, LANES), lambda l, q: (l, 0, q, 0, 0))
    mat = pl.BlockSpec((1, 2, 2 * PREP_PAIRS, GROUP_DIM, LANES), lambda l, q: (l, 0, q, 0, 0))
    sq = pl.BlockSpec((1, PREP_PAIRS, PAIR_K, PAIR_K), lambda l, q: (l, q, 0, 0))
    sq_shape = jax.ShapeDtypeStruct((depth, NPAIR, PAIR_K, PAIR_K), _BF16)
    return pl.pallas_call(
        functools.partial(_wprep_kernel, nk=nk),
        grid=(depth, NPAIR // PREP_PAIRS),
        in_specs=[vec, vec, vec, mat, mat, mat, mat],
        out_specs=[sq, sq, sq,
                   pl.BlockSpec((1, PREP_PAIRS, 2, 2, nk + 1, LANES), lambda l, q: (l, q, 0, 0, 0, 0))],
        out_shape=[sq_shape, sq_shape, sq_shape,
                   jax.ShapeDtypeStruct((depth, NPAIR, 2, 2, nk + 1, LANES), _F32)],
        scratch_shapes=[pltpu.VMEM((2 * PREP_PAIRS, PAIR_K, LANES), _F32)] * 4
                       + [pltpu.VMEM((4 * PREP_PAIRS, PAIR_K, LANES), _F32)],
        compiler_params=_params(2),
        name="s5_weight_prep",
    )(row(lam_re), row(lam_im), ldt, bt(b_re), bt(b_im), dup(c_re), dup(c_im))


def _inproj_kernel(x_ref, npm_ref, win_ref, nv_ref, ws_ref, bs_ref, woa_ref,
                   ga_ref, u_ref, gb_ref, mix_ref, *, rows, tiles_per_seg, seg):
    d = D_MODEL
    prow = min(PART_ROWS, rows)
    nchunk = prow // GMLP_CHUNK
    parts = [slice(q * prow, (q + 1) * prow) for q in range(rows // prow)]
    h = [_rms(x_ref[p, :], npm_ref[0]).astype(_BF16) for p in parts]
    vn = [_rms(_gelu(_dot(hq, win_ref[0, :, d:2 * d])), nv_ref[0]).astype(_BF16) for hq in h]
    r0 = pl.multiple_of((pl.program_id(0) % tiles_per_seg) * rows, rows)
    for p, hq in zip(parts, h):
        ub = _dot(hq, win_ref[0, :, 2 * d:3 * d])
        for b in range(NBLK):
            u_ref[b, pl.ds(r0 + p.start, prow), :] = ub[:, b * LANES:(b + 1) * LANES]
        gb_ref[p, :] = _sigmoid(_dot(hq, win_ref[0, :, 4 * d:5 * d])).astype(_BF16)
    u_ref[:, seg:seg + SEG_PAD, :] = jnp.zeros((NBLK, SEG_PAD, LANES), _F32)
    ua = [_gelu(_dot(hq, win_ref[0, :, 0:d])) for hq in h]
    for p, vq in zip(parts, vn):
        for g in range(A_GROUPS):
            cs = slice(g * LANES, (g + 1) * LANES)
            v_all = jnp.concatenate([vq[n * GMLP_CHUNK:(n + 1) * GMLP_CHUNK, cs] for n in range(nchunk)], axis=1)
            mixed = _dot(ws_ref[0, g], v_all)
            for n in range(nchunk):
                mix_ref[p.start + n * GMLP_CHUNK:p.start + (n + 1) * GMLP_CHUNK, cs] = (
                    mixed[:, n * LANES:(n + 1) * LANES] + bs_ref[0, :, cs])
    for p, hq, uq in zip(parts, h, ua):
        a = _dot((uq * mix_ref[p, :]).astype(_BF16), woa_ref[0])
        ga_ref[p, :] = (_sigmoid(_dot(hq, win_ref[0, :, 3 * d:4 * d])) * a).astype(_BF16)


def _inproj(x, npm, w_in, nv, w_s, bias, w_out_a, layer, seg):
    ntok = x.shape[0]
    rows = min(INPROJ_ROWS, seg)
    tps = seg // rows
    pitch = seg + SEG_PAD
    tok = pl.BlockSpec((rows, D_MODEL), lambda i: (i, 0))
    return pl.pallas_call(
        functools.partial(_inproj_kernel, rows=rows, tiles_per_seg=tps, seg=seg),
        grid=(ntok // rows,),
        in_specs=[tok] + [_layer_spec(w.shape, layer) for w in (npm, w_in, nv, w_s, bias, w_out_a)],
        out_specs=[tok, pl.BlockSpec((NBLK, pitch, LANES), lambda i: (0, i // tps, 0)), tok],
        out_shape=[jax.ShapeDtypeStruct((ntok, D_MODEL), _BF16),
                   jax.ShapeDtypeStruct((NBLK, ntok // seg * pitch, LANES), _F32),
                   jax.ShapeDtypeStruct((ntok, D_MODEL), _BF16)],
        scratch_shapes=[pltpu.VMEM((rows, D_MODEL), _F32)],
        compiler_params=_params(1),
        name="inproj_gmlp",
    )(x, npm, w_in, nv, w_s, bias, w_out_a)


def _s5_kernel(u_ref, t_ref, wi_ref, wo_ref, pw_ref, skip_ref, y_ref,
               lhs_ref, z_ref, s_ref, yp_ref, *, nk, pitch):
    ns = BLK_STATE
    npair = PAIRS_PER_BLK
    tgroups = CHUNK_L // npair

    def seg_rows(k, tau):
        return pl.ds(k * CHUNK_L + tau, NSEG, stride=pitch)

    def power_row(d, c, row):
        return jnp.concatenate([pw_ref[0, j, d, c, row, :] for j in range(npair)], axis=1)

    def pick(gran, pieces, base):
        out = pieces[(npair - 1 - base) % npair]
        for g in range(npair - 2, -1, -1):
            out = jnp.where(gran == g, pieces[(g - base) % npair], out)
        return out

    kspan = nk // ROW_SPLITS
    row_span = [slice(h * kspan * SUBLANES, (h + 1) * kspan * SUBLANES) for h in range(ROW_SPLITS)]

    def gather(tg, h):
        def body(i, _):
            r = pl.ds(pl.multiple_of(i * ROW_BLK, ROW_BLK), ROW_BLK)
            gran = lax.broadcasted_iota(jnp.int32, (ROW_BLK, LANES), 1) >> (PAIR_LANES.bit_length() - 1)
            w = []
            for tl in range(npair):
                v = jnp.concatenate([u_ref[seg_rows(i * KPB + kk, tg * npair + tl), :] for kk in range(KPB)],
                                    axis=0)
                w.append(pltpu.roll(v, tl * PAIR_LANES, 1) if tl else v)
            for j in range(npair):
                lhs_ref[j, r, tg * LANES:(tg + 1) * LANES] = pick(gran, w, j).astype(_BF16)
            return 0

        lax.fori_loop(h * kspan // KPB, (h + 1) * kspan // KPB, body, 0, unroll=True)

    seg_id = lax.broadcasted_iota(jnp.int32, (NSEG, ns), 0)
    for h, rs in enumerate(row_span):
        for tg in range(tgroups):
            gather(tg, h)
        for d in range(2):
            for j in range(PAIRS_PER_BLK):
                zz = _dot(lhs_ref[j, rs, :], wi_ref[0, j, :, d * 2 * LANES:(d + 1) * 2 * LANES])
                z_ref[d, rs, j * LANES:(j + 1) * LANES] = zz[:, 0:LANES]
                z_ref[d, rs, ns + j * LANES:ns + (j + 1) * LANES] = zz[:, LANES:2 * LANES]
        for j in range(PAIRS_PER_BLK):
            yp_ref[j, rs, :] = _dot(lhs_ref[j, rs, :], t_ref[0, j])

    for d in range(2):
        backward = d == 1
        al_r = jnp.broadcast_to(power_row(d, 0, slice(1, 2)), (NSEG, ns))
        al_i = jnp.broadcast_to(power_row(d, 1, slice(1, 2)), (NSEG, ns))

        def local_scan(j, carry):
            k = nk - 1 - j if backward else j
            r = pl.ds(pl.multiple_of(k * SUBLANES, SUBLANES), SUBLANES)
            sr, si = carry
            zr, zi = z_ref[d, r, 0:ns], z_ref[d, r, ns:2 * ns]
            z_ref[d, r, 0:ns] = sr
            z_ref[d, r, ns:2 * ns] = si
            pr, pi = _cmul(al_r, al_i, sr, si)
            return pr + zr, pi + zi

        zero = jnp.zeros((NSEG, ns), _F32)
        tot_r, tot_i = lax.fori_loop(0, nk, local_scan, (zero, zero), unroll=SCAN_UNROLL)

        as_r = jnp.broadcast_to(power_row(d, 0, slice(nk, nk + 1)), (NSEG, ns))
        as_i = jnp.broadcast_to(power_row(d, 1, slice(nk, nk + 1)), (NSEG, ns))
        edge = NSEG - 1 if backward else 0
        shift = NSEG - 1 if backward else 1
        cr, ci = zero, zero
        for _ in range(NSEG - 1):
            pr, pi = _cmul(as_r, as_i, cr, ci)
            cr = jnp.where(seg_id == edge, 0.0, pltpu.roll(pr + tot_r, shift, 0))
            ci = jnp.where(seg_id == edge, 0.0, pltpu.roll(pi + tot_i, shift, 0))

        def add_carry(i, _):
            r = pl.ds(pl.multiple_of(i * ROW_BLK, ROW_BLK), ROW_BLK)
            sr, si = [], []
            for kk in range(KPB):
                k = i * KPB + kk
                e = nk - 1 - k if backward else k
                pr, pi = _cmul(power_row(d, 0, pl.ds(e, 1)), power_row(d, 1, pl.ds(e, 1)), cr, ci)
                rk = pl.ds(pl.multiple_of(k * SUBLANES, SUBLANES), SUBLANES)
                sr.append(z_ref[d, rk, 0:ns] + pr)
                si.append(z_ref[d, rk, ns:2 * ns] + pi)
            s_ref[r, d * 2 * ns:d * 2 * ns + ns] = jnp.concatenate(sr, axis=0).astype(_BF16)
            s_ref[r, d * 2 * ns + ns:(d + 1) * 2 * ns] = jnp.concatenate(si, axis=0).astype(_BF16)
            return 0

        lax.fori_loop(0, nk // KPB, add_carry, 0, unroll=SCAN_UNROLL)

    skip = skip_ref[0, 0]

    def scatter(tg, h):
        def body(k, _):
            r = pl.ds(pl.multiple_of(k * SUBLANES, SUBLANES), SUBLANES)
            gran = lax.broadcasted_iota(jnp.int32, (SUBLANES, LANES), 1) >> (PAIR_LANES.bit_length() - 1)
            yb = [yp_ref[j, r, tg * LANES:(tg + 1) * LANES] for j in range(npair)]
            for tl in range(npair):
                v = pick(gran, yb, tl)
                v = pltpu.roll(v, LANES - tl * PAIR_LANES, 1) if tl else v
                rows = seg_rows(k, tg * npair + tl)
                y_ref[rows, :] = v + skip * u_ref[rows, :]
            return 0

        lax.fori_loop(h * kspan, (h + 1) * kspan, body, 0, unroll=True)

    for h, rs in enumerate(row_span):
        for j in range(PAIRS_PER_BLK):
            sp = jnp.concatenate([s_ref[rs, q * ns + j * LANES:q * ns + (j + 1) * LANES] for q in range(4)],
                                 axis=1)
            yp_ref[j, rs, :] = yp_ref[j, rs, :] + _dot(sp, wo_ref[0, j])
        for tg in range(tgroups):
            scatter(tg, h)
    seg = nk * CHUNK_L
    for s in range(NSEG):
        y_ref[s * pitch + seg:(s + 1) * pitch, :] = jnp.zeros((SEG_PAD, LANES), _F32)


def _s5_core(u_pad, t_mat, w_si, w_so, pw, skip, layer, nseq, seg):
    nk = seg // CHUNK_L
    pitch = seg + SEG_PAD
    m = nk * NSEG
    seq_blk = pl.BlockSpec((None, NSEG * pitch, LANES), lambda b, s: (b, s, 0))
    sq = pl.BlockSpec((1, PAIRS_PER_BLK, PAIR_K, PAIR_K), lambda b, s: (layer, b, 0, 0))
    return pl.pallas_call(
        functools.partial(_s5_kernel, nk=nk, pitch=pitch),
        grid=(NBLK, nseq),
        in_specs=[seq_blk, sq, sq, sq,
                  pl.BlockSpec((1, PAIRS_PER_BLK, 2, 2, nk + 1, LANES), lambda b, s: (layer, b, 0, 0, 0, 0)),
                  pl.BlockSpec((1, 1, 1, LANES), lambda b, s: (layer, b, 0, 0))],
        out_specs=seq_blk,
        out_shape=jax.ShapeDtypeStruct(u_pad.shape, _F32),
        scratch_shapes=[pltpu.VMEM((PAIRS_PER_BLK, m, PAIR_K), _BF16),
                        pltpu.VMEM((2, m, 2 * BLK_STATE), _F32),
                        pltpu.VMEM((m, 4 * BLK_STATE), _BF16),
                        pltpu.VMEM((PAIRS_PER_BLK, m, PAIR_K), _F32)],
        compiler_params=_params(2),
        name="s5_core",
    )(u_pad, t_mat, w_si, w_so, pw, skip)


def _merge_ffn_kernel(y_ref, ga_ref, gb_ref, x_ref, wglu_ref, wo_ref, npost_ref,
                      npre_ref, w1_ref, w2_ref, npostff_ref, o_ref, *, rows, tiles_per_seg):
    d = D_MODEL
    r0 = pl.multiple_of((pl.program_id(0) % tiles_per_seg) * rows, rows)
    prow = min(PART_ROWS, rows)
    parts = [slice(q * prow, (q + 1) * prow) for q in range(rows // prow)]
    def y_rows(p):
        return jnp.concatenate([y_ref[b, pl.ds(r0 + p.start, p.stop - p.start), :] for b in range(NBLK)], axis=1)

    z = [_gelu(y_rows(p)).astype(_BF16) for p in parts]
    g1 = [_dot(zq, wglu_ref[0, :, 0:d]) for zq in z]
    g2 = [_dot(zq, wglu_ref[0, :, d:2 * d]) for zq in z]
    m = [(ga_ref[p, :].astype(_F32) + gb_ref[p, :].astype(_F32) * (a * _sigmoid(b))).astype(_BF16)
         for p, a, b in zip(parts, g1, g2)]
    x1 = [x_ref[p, :] + _rms(_dot(mq, wo_ref[0]), npost_ref[0]) for p, mq in zip(parts, m)]
    h = [_rms(xq, npre_ref[0]).astype(_BF16) for xq in x1]
    f = [jnp.zeros(xq.shape, _F32) for xq in x1]
    for j in range(D_FF // D_MODEL):
        cs = slice(j * D_MODEL, (j + 1) * D_MODEL)
        for q, hq in enumerate(h):
            r = jnp.maximum(_dot(hq, w1_ref[0, :, cs]), 0.0)
            f[q] = f[q] + _dot((r * r).astype(_BF16), w2_ref[0, cs, :])
    for p, xq, fq in zip(parts, x1, f):
        o_ref[p, :] = xq + _rms(fq, npostff_ref[0])


def _merge_ffn(y_pad, ga, gb, x, w_glu, w_o, npost, npre, w1, w2, npostff, layer, seg):
    ntok = x.shape[0]
    rows = min(TOK_ROWS, seg)
    tps = seg // rows
    pitch = seg + SEG_PAD
    tok = pl.BlockSpec((rows, D_MODEL), lambda i: (i, 0))
    return pl.pallas_call(
        functools.partial(_merge_ffn_kernel, rows=rows, tiles_per_seg=tps),
        grid=(ntok // rows,),
        in_specs=[pl.BlockSpec((NBLK, pitch, LANES), lambda i: (0, i // tps, 0)), tok, tok, tok]
                 + [_layer_spec(w.shape, layer) for w in (w_glu, w_o, npost, npre, w1, w2, npostff)],
        out_specs=tok,
        out_shape=jax.ShapeDtypeStruct((ntok, D_MODEL), _F32),
        compiler_params=pltpu.CompilerParams(dimension_semantics=("arbitrary",),
                                             vmem_limit_bytes=FUSED_VMEM_LIMIT),
        name="merge_ffn",
    )(y_pad, ga, gb, x, w_glu, w_o, npost, npre, w1, w2, npostff)


def _trunk(x, p, s5m):
    nseq, seqlen, d = x.shape
    seg = seqlen // NSEG
    assert d == D_MODEL and seg % GMLP_CHUNK == 0 and seqlen % TOK_ROWS == 0
    x = x.reshape(nseq * seqlen, d)
    t_mat, w_si, w_so, pw = s5m
    for l in range(p["w_in"].shape[0]):
        ga, u_pad, gb = _inproj(x, p["norm_pre_mix"], p["w_in"], p["norm_v"], p["w_s"], p["bias"],
                                p["w_out_a"], l, seg)
        y_pad = _s5_core(u_pad, t_mat, w_si, w_so, pw, p["d_skip"], l, nseq, seg)
        x = _merge_ffn(y_pad, ga, gb, x, p["w_glu"], p["w_o"], p["norm_post_mix"],
                       p["norm_pre_ff"], p["w_ff1"], p["w_ff2"], p["norm_post_ff"], l, seg)
    return x.reshape(nseq, seqlen, d)


def kernel(x_prompt, x_sample, norm_pre_mix, w_in, norm_v, w_s, b_s, w_out_a, lam_re, lam_im,
           log_dt, b_re, b_im, c_re, c_im, d_skip, w_glu, w_o, norm_post_mix, norm_pre_ff,
           w_ff1, w_ff2, norm_post_ff):
    depth = w_in.shape[0]
    bf = lambda w: w.astype(_BF16)
    row = lambda v: v.reshape(depth, 1, D_MODEL)
    p = {
        "norm_pre_mix": row(norm_pre_mix), "w_in": bf(w_in), "norm_v": row(norm_v), "w_s": bf(w_s),
        "bias": jnp.repeat(jnp.swapaxes(b_s, 1, 2), LANES, axis=2),
        "w_out_a": bf(w_out_a), "d_skip": d_skip.reshape(depth, NBLK, 1, LANES),
        "w_glu": bf(w_glu), "w_o": bf(w_o), "norm_post_mix": row(norm_post_mix),
        "norm_pre_ff": row(norm_pre_ff), "w_ff1": bf(w_ff1), "w_ff2": bf(w_ff2),
        "norm_post_ff": row(norm_post_ff),
    }
    outs = []
    mats = {}
    for x in (x_prompt, x_sample):
        nk = x.shape[1] // (NSEG * CHUNK_L)
        if nk not in mats:
            mats[nk] = _s5_matrices(lam_re, lam_im, log_dt, b_re, b_im, c_re, c_im, nk)
        outs.append(_trunk(x, p, mats[nk]))
    return tuple(outs)
```
